```python
import math
import jax, jax.numpy as jnp
from jax import lax
import numpy as np

D_MODEL = 1024
BATCH = 2
SEQ = 8192
DEPTH = 1

CHUNK = 64
N_META = 16
PAD = CHUNK - N_META
NORM_EPS = 1e-6

SSD_EXPAND = 2
SSD_D_INNER = SSD_EXPAND * D_MODEL
SSD_HEAD_DIM = 64
SSD_HEADS = SSD_D_INNER // SSD_HEAD_DIM
SSD_GROUPS = 4
SSD_STATE = 128
SSD_CONV = 4
SSD_CONV_DIM = SSD_D_INNER + 2 * SSD_GROUPS * SSD_STATE

DN_QK_HEADS = 8
DN_V_HEADS = 16
DN_HEAD_K = 128
DN_HEAD_V = 128
DN_KEY_DIM = DN_QK_HEADS * DN_HEAD_K
DN_VALUE_DIM = DN_V_HEADS * DN_HEAD_V
DN_CONV = 4
DN_CONV_DIM = 2 * DN_KEY_DIM + DN_VALUE_DIM

N_BRANCH = 2
BRANCH_WIDTH = SSD_D_INNER

D_FF = -(-8 * D_MODEL // (3 * 256)) * 256

IN_SPLITS = (SSD_D_INNER, SSD_CONV_DIM, SSD_HEADS, DN_CONV_DIM, DN_V_HEADS, DN_V_HEADS,
             DN_VALUE_DIM, N_BRANCH * D_MODEL)
D_IN_PROJ = sum(IN_SPLITS)

kernel_name = "hybrid_ssd_gated_deltanet_block"


def rmsnorm(x, w):
    xf = x.astype(jnp.float32)
    y = xf * lax.rsqrt(jnp.mean(xf * xf, axis=-1, keepdims=True) + NORM_EPS)
    return (y * w.astype(jnp.float32)).astype(x.dtype)


def group_rms(y, groups):
    shp = y.shape
    yg = y.reshape(*shp[:-1], groups, shp[-1] // groups)
    yg = yg * lax.rsqrt(jnp.mean(yg * yg, axis=-1, keepdims=True) + NORM_EPS)
    return yg.reshape(shp)


def l2norm(x):
    return x * lax.rsqrt(jnp.sum(x * x, axis=-1, keepdims=True) + NORM_EPS)


def causal_depthwise_conv(x, w):
    k, c = w.shape
    return lax.conv_general_dilated(
        x, w[:, None, :].astype(x.dtype), window_strides=(1,), padding=[(k - 1, 0)],
        dimension_numbers=("NWC", "WIO", "NWC"), feature_group_count=c)


def segsum_exp(a_cum):
    l = a_cum.shape[-1]
    mask = jnp.tril(jnp.ones((l, l), dtype=bool))
    diff = a_cum[..., :, None] - a_cum[..., None, :]
    return jnp.exp(jnp.where(mask, diff, -jnp.inf))


def ssd_chunked(x, dt, a_head, bm, cm):
    b, t, h, p = x.shape
    g, n = bm.shape[-2:]
    c = t // CHUNK
    hg = h // g
    xdt = (x * dt[..., None]).reshape(b, c, CHUNK, g, hg, p)
    a_cum = jnp.cumsum(jnp.moveaxis((dt * a_head).reshape(b, c, CHUNK, g, hg), 2, -1), axis=-1)
    bc = bm.reshape(b, c, CHUNK, g, n)
    cc = cm.reshape(b, c, CHUNK, g, n)
    cb = jnp.einsum("bclgn,bcsgn->bcgls", cc, bc)
    scores = cb[:, :, :, None] * segsum_exp(a_cum)
    y_diag = jnp.einsum("bcgjls,bcsgjp->bclgjp", scores, xdt)
    decay_to_end = jnp.moveaxis(jnp.exp(a_cum[..., -1:] - a_cum), -1, 2)[..., None]
    states = jnp.einsum("bclgn,bclgjp->bcgjpn", bc, xdt * decay_to_end)
    chunk_decay = jnp.exp(a_cum[..., -1])

    def step(s, inp):
        st, dec = inp
        return s * dec[..., None, None] + st, s

    s0 = jnp.zeros((b, g, hg, p, n), jnp.float32)
    _, prev = lax.scan(step, s0, (jnp.moveaxis(states, 1, 0), jnp.moveaxis(chunk_decay, 1, 0)))
    prev = jnp.moveaxis(prev, 0, 1)
    y_off = jnp.einsum("bclgn,bcgjpn->bclgjp", cc, prev) * jnp.moveaxis(jnp.exp(a_cum), -1, 2)[..., None]
    return (y_diag + y_off).reshape(b, t, h, p)


def gated_delta_chunked(q, k, v, g, beta):
    b, t, h, kd = q.shape
    vd = v.shape[-1]
    c = t // CHUNK

    def chunks(a):
        return jnp.moveaxis(a.reshape(b, c, CHUNK, h, *a.shape[3:]), 3, 2)

    q, k, v, g, beta = chunks(q), chunks(k), chunks(v), chunks(g), chunks(beta)
    g_cum = jnp.cumsum(g, axis=-1)
    decay = segsum_exp(g_cum)
    k_beta = k * beta[..., None]
    strict = jnp.tril(jnp.ones((CHUNK, CHUNK), dtype=bool), -1)
    a_low = jnp.where(strict, jnp.einsum("bchlk,bchsk->bchls", k_beta, k) * decay, 0.0)
    eye = jnp.eye(CHUNK, dtype=jnp.float32)
    t_inv = lax.linalg.triangular_solve(a_low + eye, jnp.broadcast_to(eye, a_low.shape),
                                        left_side=True, lower=True, unit_diagonal=True)
    u = t_inv @ (v * beta[..., None])
    w = t_inv @ (k_beta * jnp.exp(g_cum)[..., None])
    qk = jnp.einsum("bchlk,bchsk->bchls", q, k) * decay
    q_dec = q * jnp.exp(g_cum)[..., None]
    k_dec = k * jnp.exp(g_cum[..., -1:] - g_cum)[..., None]
    last = jnp.exp(g_cum[..., -1])

    def step(s, inp):
        qk_i, u_i, w_i, qd_i, kd_i, last_i = inp
        v_new = u_i - jnp.einsum("bhlk,bhkv->bhlv", w_i, s)
        o = jnp.einsum("bhlk,bhkv->bhlv", qd_i, s) + jnp.einsum("bhls,bhsv->bhlv", qk_i, v_new)
        s = s * last_i[..., None, None] + jnp.einsum("bhlk,bhlv->bhkv", kd_i, v_new)
        return s, o

    s0 = jnp.zeros((b, h, kd, vd), jnp.float32)
    xs = (jnp.moveaxis(qk, 1, 0), jnp.moveaxis(u, 1, 0), jnp.moveaxis(w, 1, 0),
          jnp.moveaxis(q_dec, 1, 0), jnp.moveaxis(k_dec, 1, 0), jnp.moveaxis(last, 1, 0))
    _, o = lax.scan(step, s0, xs)
    o = jnp.moveaxis(jnp.moveaxis(o, 0, 1), 3, 2)
    return o.reshape(b, t, h, vd)


def hybrid_mixer(u, w_in, ssd_conv_w, ssd_conv_b, ssd_dt_bias, ssd_a_log, ssd_d, ssd_norm_w,
                 dn_conv_w, dn_dt_bias, dn_a_log, dn_norm_w, w_branch, w_out):
    f32 = jnp.float32
    b, l_in, _ = u.shape
    t = l_in + PAD
    up = jnp.pad(u, ((0, 0), (PAD, 0), (0, 0)))
    valid = (jnp.arange(t) >= PAD).astype(f32)[None, :, None]
    proj = up @ w_in
    z_s, xbc, dt_raw, qkv, a_raw, b_raw, z_d, gate_raw = jnp.split(
        proj, np.cumsum(IN_SPLITS)[:-1].tolist(), axis=-1)

    xbc = jax.nn.silu(causal_depthwise_conv(xbc, ssd_conv_w) + ssd_conv_b)
    xs, bm, cm = jnp.split(xbc, [SSD_D_INNER, SSD_D_INNER + SSD_GROUPS * SSD_STATE], axis=-1)
    dt = jax.nn.softplus(dt_raw.astype(f32) + ssd_dt_bias.astype(f32)) * valid
    a_head = -jnp.exp(ssd_a_log.astype(f32))
    xh = xs.astype(f32).reshape(b, t, SSD_HEADS, SSD_HEAD_DIM)
    y_s = ssd_chunked(xh, dt, a_head,
                      bm.astype(f32).reshape(b, t, SSD_GROUPS, SSD_STATE),
                      cm.astype(f32).reshape(b, t, SSD_GROUPS, SSD_STATE))
    y_s = (y_s + ssd_d.astype(f32)[:, None] * xh).reshape(b, t, SSD_D_INNER)
    y_s = group_rms(y_s * jax.nn.silu(z_s.astype(f32)), SSD_GROUPS) * ssd_norm_w.astype(f32)

    qkv = jax.nn.silu(causal_depthwise_conv(qkv, dn_conv_w))
    q, k, v = jnp.split(qkv, [DN_KEY_DIM, 2 * DN_KEY_DIM], axis=-1)
    rep = DN_V_HEADS // DN_QK_HEADS
    q = jnp.repeat(l2norm(q.astype(f32).reshape(b, t, DN_QK_HEADS, DN_HEAD_K)) * (DN_HEAD_K ** -0.5), rep, axis=2)
    k = jnp.repeat(l2norm(k.astype(f32).reshape(b, t, DN_QK_HEADS, DN_HEAD_K)), rep, axis=2)
    v = v.astype(f32).reshape(b, t, DN_V_HEADS, DN_HEAD_V)
    beta = jax.nn.sigmoid(b_raw.astype(f32)) * valid
    g = -jnp.exp(dn_a_log.astype(f32)) * jax.nn.softplus(a_raw.astype(f32) + dn_dt_bias.astype(f32)) * valid
    o = gated_delta_chunked(q, k, v, g, beta)
    y_d = (group_rms(o, 1) * dn_norm_w.astype(f32)).reshape(b, t, DN_VALUE_DIM) * jax.nn.silu(z_d.astype(f32))

    ys = jnp.stack([y_s, y_d], axis=2)[:, PAD:].astype(u.dtype)
    br = jnp.einsum("blnc,ncd->blnd", ys, w_branch)
    gates = jax.nn.sigmoid(gate_raw[:, PAD:].reshape(b, l_in, N_BRANCH, D_MODEL))
    merged = jnp.sum(gates * br, axis=2)
    return merged @ w_out


def swiglu(u, w_gate_up, w_down):
    gt, up = jnp.split(u @ w_gate_up, 2, axis=-1)
    return (jax.nn.silu(gt) * up) @ w_down


def _inv_softplus_dt(key, shape):
    dt = jnp.exp(jax.random.uniform(key, shape, jnp.float32) * (math.log(0.1) - math.log(0.001)) + math.log(0.001))
    return dt + jnp.log(-jnp.expm1(-dt))


def setup_inputs(seed: int = 0) -> dict:
    key = jax.random.key(seed)
    ks = jax.random.split(key, 24)
    nrm = lambda k, s, sc: jax.random.normal(k, s, jnp.float32) * sc
    gain = lambda k, s: 1.0 + 0.02 * jax.random.normal(k, s, jnp.float32)
    return {
        "x": nrm(ks[0], (BATCH, SEQ, D_MODEL), 1.0),
        "meta_tokens": nrm(ks[1], (N_META, D_MODEL), 1.0),
        "mix_norm_w": gain(ks[2], (DEPTH, D_MODEL)),
        "w_in": nrm(ks[3], (DEPTH, D_MODEL, D_IN_PROJ), D_MODEL ** -0.5),
        "ssd_conv_w": nrm(ks[4], (DEPTH, SSD_CONV, SSD_CONV_DIM), SSD_CONV ** -0.5),
        "ssd_conv_b": nrm(ks[5], (DEPTH, SSD_CONV_DIM), 0.02),
        "ssd_dt_bias": _inv_softplus_dt(ks[6], (DEPTH, SSD_HEADS)),
        "ssd_a_log": jnp.log(jax.random.uniform(ks[7], (DEPTH, SSD_HEADS), jnp.float32, 1.0, 16.0)),
        "ssd_d": gain(ks[8], (DEPTH, SSD_HEADS)),
        "ssd_norm_w": gain(ks[9], (DEPTH, SSD_D_INNER)),
        "dn_conv_w": nrm(ks[10], (DEPTH, DN_CONV, DN_CONV_DIM), DN_CONV ** -0.5),
        "dn_dt_bias": _inv_softplus_dt(ks[11], (DEPTH, DN_V_HEADS)),
        "dn_a_log": jnp.log(jax.random.uniform(ks[12], (DEPTH, DN_V_HEADS), jnp.float32, 1.0, 16.0)),
        "dn_norm_w": gain(ks[13], (DEPTH, DN_HEAD_V)),
        "w_branch": nrm(ks[14], (DEPTH, N_BRANCH, BRANCH_WIDTH, D_MODEL), BRANCH_WIDTH ** -0.5),
        "w_out": nrm(ks[15], (DEPTH, D_MODEL, D_MODEL), D_MODEL ** -0.5),
        "ffn_norm_w": gain(ks[16], (DEPTH, D_MODEL)),
        "w_gate_up": nrm(ks[17], (DEPTH, D_MODEL, 2 * D_FF), D_MODEL ** -0.5),
        "w_down": nrm(ks[18], (DEPTH, D_FF, D_MODEL), D_FF ** -0.5),
        "final_norm_w": gain(ks[19], (D_MODEL,)),
    }


def reference(x, meta_tokens, mix_norm_w, w_in, ssd_conv_w, ssd_conv_b, ssd_dt_bias, ssd_a_log,
              ssd_d, ssd_norm_w, dn_conv_w, dn_dt_bias, dn_a_log, dn_norm_w, w_branch, w_out,
              ffn_norm_w, w_gate_up, w_down, final_norm_w):
    b = x.shape[0]
    meta = jnp.broadcast_to(meta_tokens.astype(x.dtype)[None], (b, N_META, D_MODEL))
    h = jnp.concatenate([meta, x], axis=1)
    for i in range(DEPTH):
        h = h + hybrid_mixer(rmsnorm(h, mix_norm_w[i]), w_in[i], ssd_conv_w[i], ssd_conv_b[i],
                             ssd_dt_bias[i], ssd_a_log[i], ssd_d[i], ssd_norm_w[i], dn_conv_w[i],
                             dn_dt_bias[i], dn_a_log[i], dn_norm_w[i], w_branch[i], w_out[i])
        h = h + swiglu(rmsnorm(h, ffn_norm_w[i]), w_gate_up[i], w_down[i])
    return rmsnorm(h, final_norm_w)[:, N_META:]
```

```python
import functools

import numpy as np
import jax
import jax.numpy as jnp
from jax import lax
from jax.experimental import pallas as pl
from jax.experimental.pallas import tpu as pltpu

F32 = jnp.float32
BF16 = jnp.bfloat16

D_MODEL = 1024
CHUNK = 64
N_META = 16
PAD = CHUNK - N_META
NORM_EPS = 1e-6

SSD_D_INNER = 2048
SSD_HEAD_DIM = 64
SSD_HEADS = 32
SSD_GROUPS = 4
SSD_STATE = 128
SSD_CONV = 4
SSD_CONV_DIM = 3072
SSD_GROUP_W = SSD_D_INNER // SSD_GROUPS

DN_QK_HEADS = 8
DN_V_HEADS = 16
DN_HEAD = 128
DN_KEY_DIM = 1024
DN_VALUE_DIM = 2048
DN_CONV = 4
DN_CONV_DIM = 4096

D_FF = 2816
SMALL_W = 128
COL_DT, COL_A, COL_B = 0, 32, 48
MAIN_W = SSD_D_INNER + DN_VALUE_DIM + DN_CONV_DIM + SSD_CONV_DIM
CONV_HALO = 8
VMEM_LIMIT = 56 * 1024 * 1024


def _silu(v):
    return v * jax.nn.sigmoid(v)


def _dot(a, b):
    return jnp.dot(a, b, preferred_element_type=F32)


def _dot_nt(a, b):
    return lax.dot_general(a, b, (((1,), (1,)), ((), ())), preferred_element_type=F32)


def _split3(v):
    p1 = v.astype(BF16)
    r1 = v - p1.astype(F32)
    p2 = r1.astype(BF16)
    p3 = (r1 - p2.astype(F32)).astype(BF16)
    return p1, p2, p3


def _dot_sel_right(sel, v):
    p1, p2, p3 = _split3(v)
    return _dot(sel, p1) + _dot(sel, p2) + _dot(sel, p3)


def _dot_sel_left(v, sel):
    p1, p2, p3 = _split3(v)
    return _dot(p1, sel) + _dot(p2, sel) + _dot(p3, sel)


def _inproj_kernel(h_ref, nw_ref, w_ref, ws_ref, o_ref, os_ref, u_scr):
    @pl.when(pl.program_id(1) == 0)
    def _():
        x = h_ref[...]
        ms = jnp.mean(x * x, axis=-1, keepdims=True)
        u = (x * lax.rsqrt(ms + NORM_EPS) * nw_ref[...]).astype(BF16)
        u_scr[...] = u
        os_ref[...] = _dot(u, ws_ref[...])

    o_ref[...] = _dot(u_scr[...], w_ref[...]).astype(BF16)


def _inproj(h_pad, norm_w, w_main, w_small, tm, tn):
    m = h_pad.shape[0]
    return pl.pallas_call(
        _inproj_kernel,
        out_shape=(jax.ShapeDtypeStruct((m, MAIN_W), BF16),
                   jax.ShapeDtypeStruct((m, SMALL_W), F32)),
        grid=(m // tm, MAIN_W // tn),
        in_specs=[pl.BlockSpec((tm, D_MODEL), lambda i, j: (i, 0)),
                  pl.BlockSpec((1, D_MODEL), lambda i, j: (0, 0)),
                  pl.BlockSpec((D_MODEL, tn), lambda i, j: (0, j)),
                  pl.BlockSpec((D_MODEL, SMALL_W), lambda i, j: (0, 0))],
        out_specs=(pl.BlockSpec((tm, tn), lambda i, j: (i, j)),
                   pl.BlockSpec((tm, SMALL_W), lambda i, j: (i, 0))),
        scratch_shapes=[pltpu.VMEM((tm, D_MODEL), BF16)],
        compiler_params=pltpu.CompilerParams(
            dimension_semantics=("arbitrary", "arbitrary"), vmem_limit_bytes=VMEM_LIMIT),
        name="inproj",
    )(h_pad, norm_w, w_main, w_small)


def _conv_silu(raw_refs, xe_scr, cw_ref, bias, out_scr, first):
    @pl.when(first)
    def _():
        xe_scr[0:CONV_HALO, :] = jnp.zeros((CONV_HALO, xe_scr.shape[1]), F32)

    @pl.when(jnp.logical_not(first))
    def _():
        xe_scr[0:CONV_HALO, :] = xe_scr[CHUNK:CHUNK + CONV_HALO, :]

    col = 0
    for r in raw_refs:
        w = r.shape[1]
        xe_scr[CONV_HALO:CONV_HALO + CHUNK, col:col + w] = r[...].astype(F32)
        col += w
    width = xe_scr.shape[1]
    step = 512
    for c0 in range(0, width, step):
        acc = None
        for k in range(SSD_CONV):
            lo = CONV_HALO - (SSD_CONV - 1) + k
            term = cw_ref[k:k + 1, c0:c0 + step] * xe_scr[lo:lo + CHUNK, c0:c0 + step]
            acc = term if acc is None else acc + term
        if bias is not None:
            acc = acc + bias[:, c0:c0 + step]
        out_scr[:, c0:c0 + step] = _silu(acc)


def _valid_rows(first):
    row = lax.broadcasted_iota(jnp.int32, (CHUNK, 1), 0)
    return jnp.where(jnp.logical_or(jnp.logical_not(first), row >= PAD), 1.0, 0.0).astype(F32)


def _cum_masks(width):
    li = lax.broadcasted_iota(jnp.int32, (CHUNK, width), 0)
    si = lax.broadcasted_iota(jnp.int32, (CHUNK, width), 1) & (CHUNK - 1)
    return li, si


def _block_diag(p, mask):
    return jnp.where(mask, jnp.concatenate([p, p, p, p], axis=0), 0.0).astype(BF16)


def _ssd_kernel(zs_ref, x0_ref, x1_ref, x2_ref, sm_ref, cw_ref, cb_ref, dtb_ref, alog_ref,
                dexp_ref, nw_ref, tri_ref, e_ref, y_ref, xe_scr, xbc_scr, s_scr):
    first = pl.program_id(1) == 0

    @pl.when(first)
    def _():
        s_scr[...] = jnp.zeros(s_scr.shape, F32)

    _conv_silu((x0_ref, x1_ref, x2_ref), xe_scr, cw_ref, cb_ref[...], xbc_scr, first)

    lane = lax.broadcasted_iota(jnp.int32, (1, SMALL_W), 1)
    is_dt = lane < SSD_HEADS
    sm = sm_ref[...]
    dt = jnp.where(is_dt, jax.nn.softplus(sm + dtb_ref[...]) * _valid_rows(first), 0.0)
    a = dt * jnp.where(is_dt, -jnp.exp(alog_ref[...]), 0.0)
    acum = _dot_sel_right(tri_ref[...], a)
    ex = _dot_sel_left(jnp.concatenate([dt, acum], axis=0), e_ref[...])
    dt_e = ex[0:CHUNK]
    ac_e = ex[CHUNK:2 * CHUNK]

    li, si = _cum_masks(SSD_D_INNER)
    arow = jnp.sum(jnp.where(li == si, ac_e, 0.0), axis=0, keepdims=True)
    lmat = jnp.exp(jnp.where(li >= si, ac_e - arow, -jnp.inf))
    alast = ac_e[CHUNK - 1:CHUNK, :]
    to_end = jnp.exp(alast - ac_e)
    from_start = jnp.exp(ac_e)
    chunk_decay = jnp.exp(alast)

    r4 = lax.broadcasted_iota(jnp.int32, (4 * CHUNK, 4 * CHUNK), 0) // CHUNK
    c4 = lax.broadcasted_iota(jnp.int32, (4 * CHUNK, 4 * CHUNK), 1) // CHUNK
    bd_mask = r4 == c4

    for g in range(SSD_GROUPS):
        cs = slice(g * SSD_GROUP_W, (g + 1) * SSD_GROUP_W)
        b_g = xbc_scr[:, SSD_D_INNER + g * SSD_STATE:SSD_D_INNER + (g + 1) * SSD_STATE]
        c_off = SSD_D_INNER + SSD_GROUPS * SSD_STATE
        c_g = xbc_scr[:, c_off + g * SSD_STATE:c_off + (g + 1) * SSD_STATE]
        x_g = xbc_scr[:, cs]
        c_b = c_g.astype(BF16)
        b_b = b_g.astype(BF16)
        cb_t = _dot_nt(c_b, jnp.concatenate([b_b] * 8, axis=0))
        scores = cb_t * lmat[:, cs]
        xdt = x_g * dt_e[:, cs]
        y_parts = []
        for q in range(2):
            qs = slice(q * 4 * CHUNK, (q + 1) * 4 * CHUNK)
            y_parts.append(_dot(scores[:, qs].astype(BF16), _block_diag(xdt[:, qs], bd_mask)))
        y_diag = jnp.concatenate(y_parts, axis=1)
        s_old = s_scr[:, cs]
        y_off = _dot(c_b, s_old.astype(BF16)) * from_start[:, cs]
        xw = (xdt * to_end[:, cs]).astype(BF16)
        s_scr[:, cs] = s_old * chunk_decay[:, cs] + _dot(b_g.T.astype(BF16), xw)
        y = y_diag + y_off + dexp_ref[:, cs] * x_g
        y = y * _silu(zs_ref[:, cs].astype(F32))
        ms = jnp.mean(y * y, axis=-1, keepdims=True)
        y_ref[:, cs] = (y * lax.rsqrt(ms + NORM_EPS) * nw_ref[:, cs]).astype(BF16)


def _ssd(main, small, cw, cb, dtb, alog, dexp, nw, tri, e_ssd, batch, nc):
    x_blk0 = (SSD_D_INNER + DN_VALUE_DIM + DN_CONV_DIM) // 1024
    row = lambda b, c: b * nc + c
    const = lambda b, c: (0, 0)
    out_row = lambda b, c: (b * (nc - 1) + jnp.maximum(c - 1, 0), 0)
    return pl.pallas_call(
        _ssd_kernel,
        out_shape=jax.ShapeDtypeStruct((batch * (nc - 1) * CHUNK, SSD_D_INNER), BF16),
        grid=(batch, nc),
        in_specs=[pl.BlockSpec((CHUNK, SSD_D_INNER), lambda b, c: (row(b, c), 0)),
                  pl.BlockSpec((CHUNK, 1024), lambda b, c: (row(b, c), x_blk0)),
                  pl.BlockSpec((CHUNK, 1024), lambda b, c: (row(b, c), x_blk0 + 1)),
                  pl.BlockSpec((CHUNK, 1024), lambda b, c: (row(b, c), x_blk0 + 2)),
                  pl.BlockSpec((CHUNK, SMALL_W), lambda b, c: (row(b, c), 0)),
                  pl.BlockSpec((SSD_CONV, SSD_CONV_DIM), const),
                  pl.BlockSpec((1, SSD_CONV_DIM), const),
                  pl.BlockSpec((1, SMALL_W), const),
                  pl.BlockSpec((1, SMALL_W), const),
                  pl.BlockSpec((1, SSD_D_INNER), const),
                  pl.BlockSpec((1, SSD_D_INNER), const),
                  pl.BlockSpec((CHUNK, CHUNK), const),
                  pl.BlockSpec((SMALL_W, SSD_D_INNER), const)],
        out_specs=pl.BlockSpec((CHUNK, SSD_D_INNER), out_row),
        scratch_shapes=[pltpu.VMEM((CHUNK + CONV_HALO, SSD_CONV_DIM), F32),
                        pltpu.VMEM((CHUNK, SSD_CONV_DIM), F32),
                        pltpu.VMEM((SSD_STATE, SSD_D_INNER), F32)],
        compiler_params=pltpu.CompilerParams(
            dimension_semantics=("arbitrary", "arbitrary"), vmem_limit_bytes=VMEM_LIMIT),
        name="ssd_mixer",
    )(main, main, main, main, small, cw, cb, dtb, alog, dexp, nw, tri, e_ssd)


def _l2norm(v):
    return v * lax.rsqrt(jnp.sum(v * v, axis=-1, keepdims=True) + NORM_EPS)


def _dn_kernel(zd_ref, x0_ref, x1_ref, x2_ref, x3_ref, sm_ref, cw_ref, dtb_ref, alog_ref, nw_ref,
               tri_ref, e_ref, y_ref, xe_scr, qkv_scr, s_scr):
    first = pl.program_id(1) == 0

    @pl.when(first)
    def _():
        s_scr[...] = jnp.zeros(s_scr.shape, F32)

    _conv_silu((x0_ref, x1_ref, x2_ref, x3_ref), xe_scr, cw_ref, None, qkv_scr, first)

    lane = lax.broadcasted_iota(jnp.int32, (1, SMALL_W), 1)
    is_a = jnp.logical_and(lane >= COL_A, lane < COL_A + DN_V_HEADS)
    is_b = jnp.logical_and(lane >= COL_B, lane < COL_B + DN_V_HEADS)
    sm = sm_ref[...]
    valid = _valid_rows(first)
    beta = jnp.where(is_b, jax.nn.sigmoid(sm) * valid, 0.0)
    neg_a = jnp.where(is_a, -jnp.exp(alog_ref[...]), 0.0)
    g = neg_a * jax.nn.softplus(sm + dtb_ref[...]) * valid
    gcum = _dot_sel_right(tri_ref[...], g)
    ex = _dot_sel_left(jnp.concatenate([beta, gcum], axis=0), e_ref[...])
    w64 = DN_V_HEADS * CHUNK
    beta64 = ex[0:CHUNK, 0:w64]
    beta128 = ex[0:CHUNK, w64:]
    gc64 = ex[CHUNK:, 0:w64]
    gc128 = ex[CHUNK:, w64:]

    li, si = _cum_masks(w64)
    grow = jnp.sum(jnp.where(li == si, gc64, 0.0), axis=0, keepdims=True)
    dec = jnp.exp(jnp.where(li >= si, gc64 - grow, -jnp.inf))
    li2, si2 = _cum_masks(2 * CHUNK)
    strict = li2 > si2
    glast = gc128[CHUNK - 1:CHUNK, :]
    from_start = jnp.exp(gc128)
    to_end = jnp.exp(glast - gc128)
    chunk_decay = jnp.exp(glast)

    r4 = lax.broadcasted_iota(jnp.int32, (4 * CHUNK, 4 * CHUNK), 0) // CHUNK
    c4 = lax.broadcasted_iota(jnp.int32, (4 * CHUNK, 4 * CHUNK), 1) // CHUNK
    bd_mask = r4 == c4
    li4, si4 = _cum_masks(4 * CHUNK)
    eye4 = jnp.where(li4 == si4, 1.0, 0.0).astype(F32)

    for quad in range(DN_V_HEADS // 4):
        a_parts, qkd_parts, q_heads, k_heads = [], [], [], []
        for i in (2 * quad, 2 * quad + 1):
            q_i = _l2norm(qkv_scr[:, i * DN_HEAD:(i + 1) * DN_HEAD]) * (DN_HEAD ** -0.5)
            k_i = _l2norm(qkv_scr[:, DN_KEY_DIM + i * DN_HEAD:DN_KEY_DIM + (i + 1) * DN_HEAD])
            k_b = k_i.astype(BF16)
            prod = _dot_nt(jnp.concatenate([q_i.astype(BF16), k_b], axis=0),
                           jnp.concatenate([k_b, k_b], axis=0))
            ps = slice(i * 2 * CHUNK, (i + 1) * 2 * CHUNK)
            d_pair = dec[:, ps]
            qkd_parts.append(prod[0:CHUNK] * d_pair)
            a_parts.append(jnp.where(strict, beta64[:, ps] * prod[CHUNK:] * d_pair, 0.0))
            q_heads.append(q_i)
            k_heads.append(k_i)
        n_mat = -jnp.concatenate(a_parts, axis=1)
        qkd = jnp.concatenate(qkd_parts, axis=1)
        t_mat = eye4 + n_mat
        p_mat = _dot(n_mat.astype(BF16), _block_diag(n_mat, bd_mask))
        for _ in range(4):
            w_bd = _block_diag(p_mat, bd_mask)
            both = _dot(jnp.concatenate([p_mat, t_mat], axis=0).astype(BF16), w_bd)
            p_mat = both[0:CHUNK]
            t_mat = t_mat + both[CHUNK:]
        t_mat = t_mat + _dot(t_mat.astype(BF16), _block_diag(p_mat, bd_mask))

        for j in range(4):
            h = 4 * quad + j
            hs = slice(h * DN_HEAD, (h + 1) * DN_HEAD)
            k_i = k_heads[j // 2]
            q_i = q_heads[j // 2]
            v_h = qkv_scr[:, 2 * DN_KEY_DIM + h * DN_HEAD:2 * DN_KEY_DIM + (h + 1) * DN_HEAD]
            b_h = beta128[:, hs]
            rhs = jnp.concatenate([v_h * b_h, k_i * (b_h * from_start[:, hs])], axis=1)
            t_h = t_mat[:, j * CHUNK:(j + 1) * CHUNK].astype(BF16)
            uw = _dot(t_h, rhs.astype(BF16))
            u_h = uw[:, 0:DN_HEAD]
            w_h = uw[:, DN_HEAD:]
            s_old = s_scr[h]
            s_b = s_old.astype(BF16)
            q_dec = q_i * from_start[:, hs]
            ws_qs = _dot(jnp.concatenate([w_h, q_dec], axis=0).astype(BF16), s_b)
            v_new = u_h - ws_qs[0:CHUNK]
            v_new_b = v_new.astype(BF16)
            o_h = ws_qs[CHUNK:] + _dot(qkd[:, j * CHUNK:(j + 1) * CHUNK].astype(BF16), v_new_b)
            k_dec = k_i * to_end[:, hs]
            s_scr[h] = s_old * chunk_decay[:, hs] + _dot(k_dec.T.astype(BF16), v_new_b)
            ms = jnp.mean(o_h * o_h, axis=-1, keepdims=True)
            y = o_h * lax.rsqrt(ms + NORM_EPS) * nw_ref[...]
            y_ref[:, hs] = (y * _silu(zd_ref[:, hs].astype(F32))).astype(BF16)


def _dn(main, small, cw, dtb, alog, nw, tri, e_dn, batch, nc):
    zd_blk = SSD_D_INNER // DN_VALUE_DIM
    q_blk0 = (SSD_D_INNER + DN_VALUE_DIM) // 1024
    row = lambda b, c: b * nc + c
    const = lambda b, c: (0, 0)
    out_row = lambda b, c: (b * (nc - 1) + jnp.maximum(c - 1, 0), 0)
    return pl.pallas_call(
        _dn_kernel,
        out_shape=jax.ShapeDtypeStruct((batch * (nc - 1) * CHUNK, DN_VALUE_DIM), BF16),
        grid=(batch, nc),
        in_specs=[pl.BlockSpec((CHUNK, DN_VALUE_DIM), lambda b, c: (row(b, c), zd_blk)),
                  pl.BlockSpec((CHUNK, 1024), lambda b, c: (row(b, c), q_blk0)),
                  pl.BlockSpec((CHUNK, 1024), lambda b, c: (row(b, c), q_blk0 + 1)),
                  pl.BlockSpec((CHUNK, 1024), lambda b, c: (row(b, c), q_blk0 + 2)),
                  pl.BlockSpec((CHUNK, 1024), lambda b, c: (row(b, c), q_blk0 + 3)),
                  pl.BlockSpec((CHUNK, SMALL_W), lambda b, c: (row(b, c), 0)),
                  pl.BlockSpec((DN_CONV, DN_CONV_DIM), const),
                  pl.BlockSpec((1, SMALL_W), const),
                  pl.BlockSpec((1, SMALL_W), const),
                  pl.BlockSpec((1, DN_HEAD), const),
                  pl.BlockSpec((CHUNK, CHUNK), const),
                  pl.BlockSpec((SMALL_W, e_dn.shape[1]), const)],
        out_specs=pl.BlockSpec((CHUNK, DN_VALUE_DIM), out_row),
        scratch_shapes=[pltpu.VMEM((CHUNK + CONV_HALO, DN_CONV_DIM), F32),
                        pltpu.VMEM((CHUNK, DN_CONV_DIM), F32),
                        pltpu.VMEM((DN_V_HEADS, DN_HEAD, DN_HEAD), F32)],
        compiler_params=pltpu.CompilerParams(
            dimension_semantics=("arbitrary", "arbitrary"), vmem_limit_bytes=VMEM_LIMIT),
        name="dn_mixer",
    )(main, main, main, main, main, small, cw, dtb, alog, nw, tri, e_dn)


def _merge_kernel(x_ref, ys_ref, yd_ref, nw_ref, wg_ref, wb0_ref, wb1_ref, wo_ref, o_ref):
    x = x_ref[...]
    ms = jnp.mean(x * x, axis=-1, keepdims=True)
    u = (x * lax.rsqrt(ms + NORM_EPS) * nw_ref[...]).astype(BF16)
    gates = jax.nn.sigmoid(_dot(u, wg_ref[...]))
    merged = (gates[:, 0:D_MODEL] * _dot(ys_ref[...], wb0_ref[...])
              + gates[:, D_MODEL:] * _dot(yd_ref[...], wb1_ref[...]))
    o_ref[...] = x + _dot(merged.astype(BF16), wo_ref[...])


def _merge(x2d, ys, yd, nw, wg, wb0, wb1, wo, tm):
    m = x2d.shape[0]
    const = lambda i: (0, 0)
    rows = lambda i: (i, 0)
    return pl.pallas_call(
        _merge_kernel,
        out_shape=jax.ShapeDtypeStruct((m, D_MODEL), F32),
        grid=(m // tm,),
        in_specs=[pl.BlockSpec((tm, D_MODEL), rows),
                  pl.BlockSpec((tm, SSD_D_INNER), rows),
                  pl.BlockSpec((tm, DN_VALUE_DIM), rows),
                  pl.BlockSpec((1, D_MODEL), const),
                  pl.BlockSpec((D_MODEL, 2 * D_MODEL), const),
                  pl.BlockSpec((SSD_D_INNER, D_MODEL), const),
                  pl.BlockSpec((DN_VALUE_DIM, D_MODEL), const),
                  pl.BlockSpec((D_MODEL, D_MODEL), const)],
        out_specs=pl.BlockSpec((tm, D_MODEL), rows),
        compiler_params=pltpu.CompilerParams(
            dimension_semantics=("arbitrary",), vmem_limit_bytes=VMEM_LIMIT),
        name="merge_outproj",
    )(x2d, ys, yd, nw, wg, wb0, wb1, wo)


def _ffn_kernel(h_ref, nw_ref, wgu_ref, wd_ref, fw_ref, o_ref):
    h = h_ref[...]
    ms = jnp.mean(h * h, axis=-1, keepdims=True)
    u = (h * lax.rsqrt(ms + NORM_EPS) * nw_ref[...]).astype(BF16)
    gu = _dot(u, wgu_ref[...])
    act = (_silu(gu[:, 0:D_FF]) * gu[:, D_FF:]).astype(BF16)
    h2 = h + _dot(act, wd_ref[...])
    ms2 = jnp.mean(h2 * h2, axis=-1, keepdims=True)
    o_ref[...] = h2 * lax.rsqrt(ms2 + NORM_EPS) * fw_ref[...]


def _ffn(h1, nw, wgu, wd, fw, tm):
    m = h1.shape[0]
    const = lambda i: (0, 0)
    rows = lambda i: (i, 0)
    return pl.pallas_call(
        _ffn_kernel,
        out_shape=jax.ShapeDtypeStruct((m, D_MODEL), F32),
        grid=(m // tm,),
        in_specs=[pl.BlockSpec((tm, D_MODEL), rows),
                  pl.BlockSpec((1, D_MODEL), const),
                  pl.BlockSpec((D_MODEL, 2 * D_FF), const),
                  pl.BlockSpec((D_FF, D_MODEL), const),
                  pl.BlockSpec((1, D_MODEL), const)],
        out_specs=pl.BlockSpec((tm, D_MODEL), rows),
        compiler_params=pltpu.CompilerParams(
            dimension_semantics=("arbitrary",), vmem_limit_bytes=VMEM_LIMIT),
        name="ffn_final",
    )(h1, nw, wgu, wd, fw)


def _pad_lanes(vec, offset):
    out = jnp.zeros((1, SMALL_W), F32)
    return lax.dynamic_update_slice(out, vec.astype(F32)[None, :], (0, offset))


def _expand_matrix(src_rows, group, heads):
    e = np.zeros((SMALL_W, heads * group), np.float32)
    for h in range(heads):
        e[src_rows + h, h * group:(h + 1) * group] = 1.0
    return e


def _row_tile(m, target):
    t = min(target, m)
    while m % t or t % 8:
        t -= 8
    return t


def kernel(x, meta_tokens, mix_norm_w, w_in, ssd_conv_w, ssd_conv_b, ssd_dt_bias, ssd_a_log, ssd_d,
           ssd_norm_w, dn_conv_w, dn_dt_bias, dn_a_log, dn_norm_w, w_branch, w_out, ffn_norm_w,
           w_gate_up, w_down, final_norm_w):
    batch, seq, _ = x.shape
    t_pad = PAD + N_META + seq
    nc = t_pad // CHUNK
    assert t_pad % CHUNK == 0 and mix_norm_w.shape[0] == 1

    head = jnp.concatenate([jnp.zeros((PAD, D_MODEL), x.dtype), meta_tokens.astype(x.dtype)], axis=0)
    h_pad = jnp.concatenate([jnp.broadcast_to(head[None], (batch, CHUNK, D_MODEL)), x], axis=1)
    h_pad = h_pad.reshape(batch * t_pad, D_MODEL)

    w = w_in[0]
    o_zs, o_xbc = 0, SSD_D_INNER
    o_dt = o_xbc + SSD_CONV_DIM
    o_qkv = o_dt + SSD_HEADS
    o_a = o_qkv + DN_CONV_DIM
    o_b = o_a + DN_V_HEADS
    o_zd = o_b + DN_V_HEADS
    o_gate = o_zd + DN_VALUE_DIM
    w_main = jnp.concatenate([w[:, o_zs:o_xbc], w[:, o_zd:o_gate], w[:, o_qkv:o_a], w[:, o_xbc:o_dt]],
                             axis=1).astype(BF16)
    w_small = jnp.concatenate([w[:, o_dt:o_qkv], w[:, o_a:o_zd],
                               jnp.zeros((D_MODEL, SMALL_W - SSD_HEADS - 2 * DN_V_HEADS), w.dtype)],
                              axis=1).astype(BF16)
    w_gates = w[:, o_gate:].astype(BF16)

    m_pad = batch * t_pad
    main, small = _inproj(h_pad, mix_norm_w.astype(F32), w_main, w_small,
                          _row_tile(m_pad, 384), 1024)

    tri = jnp.asarray(np.tril(np.ones((CHUNK, CHUNK), np.float32)), BF16)
    e_ssd = jnp.asarray(_expand_matrix(COL_DT, SSD_HEAD_DIM, SSD_HEADS), BF16)
    e_dn = jnp.asarray(
        np.concatenate([_expand_matrix(COL_B, CHUNK, DN_V_HEADS) + _expand_matrix(COL_A, CHUNK, DN_V_HEADS),
                        _expand_matrix(COL_B, DN_HEAD, DN_V_HEADS) + _expand_matrix(COL_A, DN_HEAD, DN_V_HEADS)],
                       axis=1), BF16)

    y_s = _ssd(main, small, ssd_conv_w[0].astype(F32), ssd_conv_b.astype(F32),
               _pad_lanes(ssd_dt_bias[0], COL_DT), _pad_lanes(ssd_a_log[0], COL_DT),
               jnp.repeat(ssd_d[0].astype(F32), SSD_HEAD_DIM)[None, :], ssd_norm_w.astype(F32),
               tri, e_ssd, batch, nc)
    y_d = _dn(main, small, dn_conv_w[0].astype(F32), _pad_lanes(dn_dt_bias[0], COL_A),
              _pad_lanes(dn_a_log[0], COL_A), dn_norm_w.astype(F32), tri, e_dn, batch, nc)

    x2d = x.reshape(batch * seq, D_MODEL)
    tm = _row_tile(batch * seq, 256)
    h1 = _merge(x2d, y_s, y_d, mix_norm_w.astype(F32), w_gates, w_branch[0, 0].astype(BF16),
                w_branch[0, 1].astype(BF16), w_out[0].astype(BF16), tm)
    out = _ffn(h1, ffn_norm_w.astype(F32), w_gate_up[0].astype(BF16), w_down[0].astype(BF16),
               final_norm_w.astype(F32)[None, :], tm)
    return out.reshape(batch, seq, D_MODEL)
```

```python
import functools

import numpy as np
import jax
import jax.numpy as jnp
from jax import lax
from jax.experimental import pallas as pl
from jax.experimental.pallas import tpu as pltpu

F32 = jnp.float32
BF16 = jnp.bfloat16

D_MODEL = 1024
CHUNK = 64
N_META = 16
PAD = CHUNK - N_META
NORM_EPS = 1e-6

SSD_D_INNER = 2048
SSD_HEAD_DIM = 64
SSD_HEADS = 32
SSD_GROUPS = 4
SSD_STATE = 128
SSD_CONV = 4
SSD_CONV_DIM = 3072
SSD_GROUP_W = SSD_D_INNER // SSD_GROUPS

DN_QK_HEADS = 8
DN_V_HEADS = 16
DN_HEAD = 128
DN_KEY_DIM = 1024
DN_VALUE_DIM = 2048
DN_CONV = 4
DN_CONV_DIM = 4096

D_FF = 2816
SMALL_W = 128
COL_DT, COL_A, COL_B = 0, 32, 48
MAIN_W = SSD_D_INNER + DN_VALUE_DIM + DN_CONV_DIM + SSD_CONV_DIM
CONV_HALO = 8
VMEM_LIMIT = 56 * 1024 * 1024


def _silu(v):
    return v * jax.nn.sigmoid(v)


def _dot(a, b):
    return jnp.dot(a, b, preferred_element_type=F32)


def _dot_nt(a, b):
    return lax.dot_general(a, b, (((1,), (1,)), ((), ())), preferred_element_type=F32)


def _split3(v):
    p1 = v.astype(BF16)
    r1 = v - p1.astype(F32)
    p2 = r1.astype(BF16)
    p3 = (r1 - p2.astype(F32)).astype(BF16)
    return p1, p2, p3


def _dot_sel_right(sel, v):
    p1, p2, p3 = _split3(v)
    return _dot(sel, p1) + _dot(sel, p2) + _dot(sel, p3)


def _dot_sel_left(v, sel):
    p1, p2, p3 = _split3(v)
    return _dot(p1, sel) + _dot(p2, sel) + _dot(p3, sel)


def _inproj_kernel(h_ref, nw_ref, w_ref, ws_ref, o_ref, os_ref, u_scr):
    @pl.when(pl.program_id(1) == 0)
    def _():
        x = h_ref[...]
        ms = jnp.mean(x * x, axis=-1, keepdims=True)
        u = (x * lax.rsqrt(ms + NORM_EPS) * nw_ref[...]).astype(BF16)
        u_scr[...] = u
        os_ref[...] = _dot(u, ws_ref[...])

    o_ref[...] = _dot(u_scr[...], w_ref[...]).astype(BF16)


def _inproj(h_pad, norm_w, w_main, w_small, tm, tn):
    m = h_pad.shape[0]
    return pl.pallas_call(
        _inproj_kernel,
        out_shape=(jax.ShapeDtypeStruct((m, MAIN_W), BF16),
                   jax.ShapeDtypeStruct((m, SMALL_W), F32)),
        grid=(m // tm, MAIN_W // tn),
        in_specs=[pl.BlockSpec((tm, D_MODEL), lambda i, j: (i, 0)),
                  pl.BlockSpec((1, D_MODEL), lambda i, j: (0, 0)),
                  pl.BlockSpec((D_MODEL, tn), lambda i, j: (0, j)),
                  pl.BlockSpec((D_MODEL, SMALL_W), lambda i, j: (0, 0))],
        out_specs=(pl.BlockSpec((tm, tn), lambda i, j: (i, j)),
                   pl.BlockSpec((tm, SMALL_W), lambda i, j: (i, 0))),
        scratch_shapes=[pltpu.VMEM((tm, D_MODEL), BF16)],
        compiler_params=pltpu.CompilerParams(
            dimension_semantics=("arbitrary", "arbitrary"), vmem_limit_bytes=VMEM_LIMIT),
        name="inproj",
    )(h_pad, norm_w, w_main, w_small)


def _conv_silu(raw_refs, xe_scr, cw_ref, bias, out_scr, first):
    @pl.when(first)
    def _():
        xe_scr[0:CONV_HALO, :] = jnp.zeros((CONV_HALO, xe_scr.shape[1]), F32)

    @pl.when(jnp.logical_not(first))
    def _():
        xe_scr[0:CONV_HALO, :] = xe_scr[CHUNK:CHUNK + CONV_HALO, :]

    col = 0
    for r in raw_refs:
        w = r.shape[1]
        xe_scr[CONV_HALO:CONV_HALO + CHUNK, col:col + w] = r[...].astype(F32)
        col += w
    width = xe_scr.shape[1]
    step = 512
    for c0 in range(0, width, step):
        acc = None
        for k in range(SSD_CONV):
            lo = CONV_HALO - (SSD_CONV - 1) + k
            term = cw_ref[k:k + 1, c0:c0 + step] * xe_scr[lo:lo + CHUNK, c0:c0 + step]
            acc = term if acc is None else acc + term
        if bias is not None:
            acc = acc + bias[:, c0:c0 + step]
        out_scr[:, c0:c0 + step] = _silu(acc)


def _valid_rows(first):
    row = lax.broadcasted_iota(jnp.int32, (CHUNK, 1), 0)
    return jnp.where(jnp.logical_or(jnp.logical_not(first), row >= PAD), 1.0, 0.0).astype(F32)


def _cum_masks(width):
    li = lax.broadcasted_iota(jnp.int32, (CHUNK, width), 0)
    si = lax.broadcasted_iota(jnp.int32, (CHUNK, width), 1) & (CHUNK - 1)
    return li, si


def _block_diag_mask():
    shift = CHUNK.bit_length() - 1
    r = lax.shift_right_logical(lax.broadcasted_iota(jnp.int32, (4 * CHUNK, 4 * CHUNK), 0), shift)
    c = lax.shift_right_logical(lax.broadcasted_iota(jnp.int32, (4 * CHUNK, 4 * CHUNK), 1), shift)
    return r == c


def _block_diag(p, mask):
    return jnp.where(mask, jnp.concatenate([p, p, p, p], axis=0), 0.0).astype(BF16)


def _ssd_kernel(zs_ref, x0_ref, x1_ref, x2_ref, sm_ref, cw_ref, cb_ref, dtb_ref, alog_ref,
                dexp_ref, nw_ref, tri_ref, e_ref, y_ref, xe_scr, xbc_scr, s_scr):
    first = pl.program_id(1) == 0

    @pl.when(first)
    def _():
        s_scr[...] = jnp.zeros(s_scr.shape, F32)

    _conv_silu((x0_ref, x1_ref, x2_ref), xe_scr, cw_ref, cb_ref[...], xbc_scr, first)

    lane = lax.broadcasted_iota(jnp.int32, (1, SMALL_W), 1)
    is_dt = lane < SSD_HEADS
    sm = sm_ref[...]
    dt = jnp.where(is_dt, jax.nn.softplus(sm + dtb_ref[...]) * _valid_rows(first), 0.0)
    a = dt * jnp.where(is_dt, -jnp.exp(alog_ref[...]), 0.0)
    acum = _dot_sel_right(tri_ref[...], a)
    ex = _dot_sel_left(jnp.concatenate([dt, acum], axis=0), e_ref[...])
    dt_e = ex[0:CHUNK]
    ac_e = ex[CHUNK:2 * CHUNK]

    li, si = _cum_masks(SSD_D_INNER)
    arow = jnp.sum(jnp.where(li == si, ac_e, 0.0), axis=0, keepdims=True)
    lmat = jnp.exp(jnp.where(li >= si, ac_e - arow, -jnp.inf))
    alast = ac_e[CHUNK - 1:CHUNK, :]
    to_end = jnp.exp(alast - ac_e)
    from_start = jnp.exp(ac_e)
    chunk_decay = jnp.exp(alast)

    bd_mask = _block_diag_mask()

    for g in range(SSD_GROUPS):
        cs = slice(g * SSD_GROUP_W, (g + 1) * SSD_GROUP_W)
        b_g = xbc_scr[:, SSD_D_INNER + g * SSD_STATE:SSD_D_INNER + (g + 1) * SSD_STATE]
        c_off = SSD_D_INNER + SSD_GROUPS * SSD_STATE
        c_g = xbc_scr[:, c_off + g * SSD_STATE:c_off + (g + 1) * SSD_STATE]
        x_g = xbc_scr[:, cs]
        c_b = c_g.astype(BF16)
        b_b = b_g.astype(BF16)
        cb_t = _dot_nt(c_b, jnp.concatenate([b_b] * 8, axis=0))
        scores = cb_t * lmat[:, cs]
        xdt = x_g * dt_e[:, cs]
        y_parts = []
        for q in range(2):
            qs = slice(q * 4 * CHUNK, (q + 1) * 4 * CHUNK)
            y_parts.append(_dot(scores[:, qs].astype(BF16), _block_diag(xdt[:, qs], bd_mask)))
        y_diag = jnp.concatenate(y_parts, axis=1)
        s_old = s_scr[:, cs]
        y_off = _dot(c_b, s_old.astype(BF16)) * from_start[:, cs]
        xw = (xdt * to_end[:, cs]).astype(BF16)
        s_scr[:, cs] = s_old * chunk_decay[:, cs] + _dot(b_g.T.astype(BF16), xw)
        y = y_diag + y_off + dexp_ref[:, cs] * x_g
        y = y * _silu(zs_ref[:, cs].astype(F32))
        ms = jnp.mean(y * y, axis=-1, keepdims=True)
        y_ref[:, cs] = (y * lax.rsqrt(ms + NORM_EPS) * nw_ref[:, cs]).astype(BF16)


def _ssd(main, small, cw, cb, dtb, alog, dexp, nw, tri, e_ssd, batch, nc):
    x_blk0 = (SSD_D_INNER + DN_VALUE_DIM + DN_CONV_DIM) // 1024
    row = lambda b, c: b * nc + c
    const = lambda b, c: (0, 0)
    out_row = lambda b, c: (b * (nc - 1) + jnp.maximum(c - 1, 0), 0)
    return pl.pallas_call(
        _ssd_kernel,
        out_shape=jax.ShapeDtypeStruct((batch * (nc - 1) * CHUNK, SSD_D_INNER), BF16),
        grid=(batch, nc),
        in_specs=[pl.BlockSpec((CHUNK, SSD_D_INNER), lambda b, c: (row(b, c), 0)),
                  pl.BlockSpec((CHUNK, 1024), lambda b, c: (row(b, c), x_blk0)),
                  pl.BlockSpec((CHUNK, 1024), lambda b, c: (row(b, c), x_blk0 + 1)),
                  pl.BlockSpec((CHUNK, 1024), lambda b, c: (row(b, c), x_blk0 + 2)),
                  pl.BlockSpec((CHUNK, SMALL_W), lambda b, c: (row(b, c), 0)),
                  pl.BlockSpec((SSD_CONV, SSD_CONV_DIM), const),
                  pl.BlockSpec((1, SSD_CONV_DIM), const),
                  pl.BlockSpec((1, SMALL_W), const),
                  pl.BlockSpec((1, SMALL_W), const),
                  pl.BlockSpec((1, SSD_D_INNER), const),
                  pl.BlockSpec((1, SSD_D_INNER), const),
                  pl.BlockSpec((CHUNK, CHUNK), const),
                  pl.BlockSpec((SMALL_W, SSD_D_INNER), const)],
        out_specs=pl.BlockSpec((CHUNK, SSD_D_INNER), out_row),
        scratch_shapes=[pltpu.VMEM((CHUNK + CONV_HALO, SSD_CONV_DIM), F32),
                        pltpu.VMEM((CHUNK, SSD_CONV_DIM), F32),
                        pltpu.VMEM((SSD_STATE, SSD_D_INNER), F32)],
        compiler_params=pltpu.CompilerParams(
            dimension_semantics=("arbitrary", "arbitrary"), vmem_limit_bytes=VMEM_LIMIT),
        name="ssd_mixer",
    )(main, main, main, main, small, cw, cb, dtb, alog, dexp, nw, tri, e_ssd)


def _l2norm(v):
    return v * lax.rsqrt(jnp.sum(v * v, axis=-1, keepdims=True) + NORM_EPS)


def _dn_kernel(zd_ref, x0_ref, x1_ref, x2_ref, x3_ref, sm_ref, cw_ref, dtb_ref, alog_ref, nw_ref,
               tri_ref, e_ref, y_ref, xe_scr, qkv_scr, s_scr):
    first = pl.program_id(1) == 0

    @pl.when(first)
    def _():
        s_scr[...] = jnp.zeros(s_scr.shape, F32)

    _conv_silu((x0_ref, x1_ref, x2_ref, x3_ref), xe_scr, cw_ref, None, qkv_scr, first)

    lane = lax.broadcasted_iota(jnp.int32, (1, SMALL_W), 1)
    is_a = jnp.logical_and(lane >= COL_A, lane < COL_A + DN_V_HEADS)
    is_b = jnp.logical_and(lane >= COL_B, lane < COL_B + DN_V_HEADS)
    sm = sm_ref[...]
    valid = _valid_rows(first)
    beta = jnp.where(is_b, jax.nn.sigmoid(sm) * valid, 0.0)
    neg_a = jnp.where(is_a, -jnp.exp(alog_ref[...]), 0.0)
    g = neg_a * jax.nn.softplus(sm + dtb_ref[...]) * valid
    gcum = _dot_sel_right(tri_ref[...], g)
    ex = _dot_sel_left(jnp.concatenate([beta, gcum], axis=0), e_ref[...])
    w64 = DN_V_HEADS * CHUNK
    beta64 = ex[0:CHUNK, 0:w64]
    beta128 = ex[0:CHUNK, w64:]
    gc64 = ex[CHUNK:, 0:w64]
    gc128 = ex[CHUNK:, w64:]

    li, si = _cum_masks(w64)
    grow = jnp.sum(jnp.where(li == si, gc64, 0.0), axis=0, keepdims=True)
    dec = jnp.exp(jnp.where(li >= si, gc64 - grow, -jnp.inf))
    li2, si2 = _cum_masks(2 * CHUNK)
    strict = li2 > si2
    glast = gc128[CHUNK - 1:CHUNK, :]
    from_start = jnp.exp(gc128)
    to_end = jnp.exp(glast - gc128)
    chunk_decay = jnp.exp(glast)

    bd_mask = _block_diag_mask()
    li4, si4 = _cum_masks(4 * CHUNK)
    eye4 = jnp.where(li4 == si4, 1.0, 0.0).astype(F32)

    nquad = DN_V_HEADS // 4
    q_heads, k_heads, a_pairs, qkd_pairs = [], [], [], []
    for i in range(DN_QK_HEADS):
        q_i = _l2norm(qkv_scr[:, i * DN_HEAD:(i + 1) * DN_HEAD]) * (DN_HEAD ** -0.5)
        k_i = _l2norm(qkv_scr[:, DN_KEY_DIM + i * DN_HEAD:DN_KEY_DIM + (i + 1) * DN_HEAD])
        k_b = k_i.astype(BF16)
        prod = _dot_nt(jnp.concatenate([q_i.astype(BF16), k_b], axis=0),
                       jnp.concatenate([k_b, k_b], axis=0))
        ps = slice(i * 2 * CHUNK, (i + 1) * 2 * CHUNK)
        d_pair = dec[:, ps]
        qkd_pairs.append(prod[0:CHUNK] * d_pair)
        a_pairs.append(jnp.where(strict, beta64[:, ps] * prod[CHUNK:] * d_pair, 0.0))
        q_heads.append(q_i)
        k_heads.append(k_i)

    n_mats = [-jnp.concatenate(a_pairs[2 * q:2 * q + 2], axis=1) for q in range(nquad)]
    qkds = [jnp.concatenate(qkd_pairs[2 * q:2 * q + 2], axis=1) for q in range(nquad)]
    t_mats = [eye4 + n for n in n_mats]
    p_mats = [_dot(n.astype(BF16), _block_diag(n, bd_mask)) for n in n_mats]
    for _ in range(4):
        boths = [_dot(jnp.concatenate([p, t], axis=0).astype(BF16), _block_diag(p, bd_mask))
                 for p, t in zip(p_mats, t_mats)]
        p_mats = [b[0:CHUNK] for b in boths]
        t_mats = [t + b[CHUNK:] for t, b in zip(t_mats, boths)]
    t_mats = [t + _dot(t.astype(BF16), _block_diag(p, bd_mask)) for p, t in zip(p_mats, t_mats)]

    heads = range(DN_V_HEADS)
    hsl = [slice(h * DN_HEAD, (h + 1) * DN_HEAD) for h in heads]
    uws = []
    for h in heads:
        k_i = k_heads[h // 2]
        v_h = qkv_scr[:, 2 * DN_KEY_DIM + h * DN_HEAD:2 * DN_KEY_DIM + (h + 1) * DN_HEAD]
        b_h = beta128[:, hsl[h]]
        rhs = jnp.concatenate([v_h * b_h, k_i * (b_h * from_start[:, hsl[h]])], axis=1)
        j = h % 4
        t_h = t_mats[h // 4][:, j * CHUNK:(j + 1) * CHUNK].astype(BF16)
        uws.append(_dot(t_h, rhs.astype(BF16)))

    s_olds = [s_scr[h] for h in heads]
    ws_qs = [_dot(jnp.concatenate([uws[h][:, DN_HEAD:], q_heads[h // 2] * from_start[:, hsl[h]]],
                                  axis=0).astype(BF16), s_olds[h].astype(BF16)) for h in heads]
    v_news = [(uws[h][:, 0:DN_HEAD] - ws_qs[h][0:CHUNK]).astype(BF16) for h in heads]
    for h in heads:
        k_dec = k_heads[h // 2] * to_end[:, hsl[h]]
        s_scr[h] = s_olds[h] * chunk_decay[:, hsl[h]] + _dot(k_dec.T.astype(BF16), v_news[h])
    for h in heads:
        j = h % 4
        o_h = ws_qs[h][CHUNK:] + _dot(qkds[h // 4][:, j * CHUNK:(j + 1) * CHUNK].astype(BF16), v_news[h])
        ms = jnp.mean(o_h * o_h, axis=-1, keepdims=True)
        y = o_h * lax.rsqrt(ms + NORM_EPS) * nw_ref[...]
        y_ref[:, hsl[h]] = (y * _silu(zd_ref[:, hsl[h]].astype(F32))).astype(BF16)


def _dn(main, small, cw, dtb, alog, nw, tri, e_dn, batch, nc):
    zd_blk = SSD_D_INNER // DN_VALUE_DIM
    q_blk0 = (SSD_D_INNER + DN_VALUE_DIM) // 1024
    row = lambda b, c: b * nc + c
    const = lambda b, c: (0, 0)
    out_row = lambda b, c: (b * (nc - 1) + jnp.maximum(c - 1, 0), 0)
    return pl.pallas_call(
        _dn_kernel,
        out_shape=jax.ShapeDtypeStruct((batch * (nc - 1) * CHUNK, DN_VALUE_DIM), BF16),
        grid=(batch, nc),
        in_specs=[pl.BlockSpec((CHUNK, DN_VALUE_DIM), lambda b, c: (row(b, c), zd_blk)),
                  pl.BlockSpec((CHUNK, 1024), lambda b, c: (row(b, c), q_blk0)),
                  pl.BlockSpec((CHUNK, 1024), lambda b, c: (row(b, c), q_blk0 + 1)),
                  pl.BlockSpec((CHUNK, 1024), lambda b, c: (row(b, c), q_blk0 + 2)),
                  pl.BlockSpec((CHUNK, 1024), lambda b, c: (row(b, c), q_blk0 + 3)),
                  pl.BlockSpec((CHUNK, SMALL_W), lambda b, c: (row(b, c), 0)),
                  pl.BlockSpec((DN_CONV, DN_CONV_DIM), const),
                  pl.BlockSpec((1, SMALL_W), const),
                  pl.BlockSpec((1, SMALL_W), const),
                  pl.BlockSpec((1, DN_HEAD), const),
                  pl.BlockSpec((CHUNK, CHUNK), const),
                  pl.BlockSpec((SMALL_W, e_dn.shape[1]), const)],
        out_specs=pl.BlockSpec((CHUNK, DN_VALUE_DIM), out_row),
        scratch_shapes=[pltpu.VMEM((CHUNK + CONV_HALO, DN_CONV_DIM), F32),
                        pltpu.VMEM((CHUNK, DN_CONV_DIM), F32),
                        pltpu.VMEM((DN_V_HEADS, DN_HEAD, DN_HEAD), F32)],
        compiler_params=pltpu.CompilerParams(
            dimension_semantics=("arbitrary", "arbitrary"), vmem_limit_bytes=VMEM_LIMIT),
        name="dn_mixer",
    )(main, main, main, main, main, small, cw, dtb, alog, nw, tri, e_dn)


def _merge_kernel(x_ref, ys_ref, yd_ref, nw_ref, wg_ref, wb0_ref, wb1_ref, wo_ref, o_ref):
    x = x_ref[...]
    ms = jnp.mean(x * x, axis=-1, keepdims=True)
    u = (x * lax.rsqrt(ms + NORM_EPS) * nw_ref[...]).astype(BF16)
    gates = jax.nn.sigmoid(_dot(u, wg_ref[...]))
    merged = (gates[:, 0:D_MODEL] * _dot(ys_ref[...], wb0_ref[...])
              + gates[:, D_MODEL:] * _dot(yd_ref[...], wb1_ref[...]))
    o_ref[...] = x + _dot(merged.astype(BF16), wo_ref[...])


def _merge(x2d, ys, yd, nw, wg, wb0, wb1, wo, tm):
    m = x2d.shape[0]
    const = lambda i: (0, 0)
    rows = lambda i: (i, 0)
    return pl.pallas_call(
        _merge_kernel,
        out_shape=jax.ShapeDtypeStruct((m, D_MODEL), F32),
        grid=(m // tm,),
        in_specs=[pl.BlockSpec((tm, D_MODEL), rows),
                  pl.BlockSpec((tm, SSD_D_INNER), rows),
                  pl.BlockSpec((tm, DN_VALUE_DIM), rows),
                  pl.BlockSpec((1, D_MODEL), const),
                  pl.BlockSpec((D_MODEL, 2 * D_MODEL), const),
                  pl.BlockSpec((SSD_D_INNER, D_MODEL), const),
                  pl.BlockSpec((DN_VALUE_DIM, D_MODEL), const),
                  pl.BlockSpec((D_MODEL, D_MODEL), const)],
        out_specs=pl.BlockSpec((tm, D_MODEL), rows),
        compiler_params=pltpu.CompilerParams(
            dimension_semantics=("arbitrary",), vmem_limit_bytes=VMEM_LIMIT),
        name="merge_outproj",
    )(x2d, ys, yd, nw, wg, wb0, wb1, wo)


def _ffn_kernel(h_ref, nw_ref, wgu_ref, wd_ref, fw_ref, o_ref):
    h = h_ref[...]
    ms = jnp.mean(h * h, axis=-1, keepdims=True)
    u = (h * lax.rsqrt(ms + NORM_EPS) * nw_ref[...]).astype(BF16)
    gu = _dot(u, wgu_ref[...])
    act = (_silu(gu[:, 0:D_FF]) * gu[:, D_FF:]).astype(BF16)
    h2 = h + _dot(act, wd_ref[...])
    ms2 = jnp.mean(h2 * h2, axis=-1, keepdims=True)
    o_ref[...] = h2 * lax.rsqrt(ms2 + NORM_EPS) * fw_ref[...]


def _ffn(h1, nw, wgu, wd, fw, tm):
    m = h1.shape[0]
    const = lambda i: (0, 0)
    rows = lambda i: (i, 0)
    return pl.pallas_call(
        _ffn_kernel,
        out_shape=jax.ShapeDtypeStruct((m, D_MODEL), F32),
        grid=(m // tm,),
        in_specs=[pl.BlockSpec((tm, D_MODEL), rows),
                  pl.BlockSpec((1, D_MODEL), const),
                  pl.BlockSpec((D_MODEL, 2 * D_FF), const),
                  pl.BlockSpec((D_FF, D_MODEL), const),
                  pl.BlockSpec((1, D_MODEL), const)],
        out_specs=pl.BlockSpec((tm, D_MODEL), rows),
        compiler_params=pltpu.CompilerParams(
            dimension_semantics=("arbitrary",), vmem_limit_bytes=VMEM_LIMIT),
        name="ffn_final",
    )(h1, nw, wgu, wd, fw)


def _pad_lanes(vec, offset):
    out = jnp.zeros((1, SMALL_W), F32)
    return lax.dynamic_update_slice(out, vec.astype(F32)[None, :], (0, offset))


def _expand_matrix(src_rows, group, heads):
    e = np.zeros((SMALL_W, heads * group), np.float32)
    for h in range(heads):
        e[src_rows + h, h * group:(h + 1) * group] = 1.0
    return e


BF16_SUBLANES = 16
INPROJ_ROWS, INPROJ_COLS = 1376, 1024
DENSE_ROWS = 256


def _row_tile(m, target):
    t = min(target, m) // BF16_SUBLANES * BF16_SUBLANES
    while m % t:
        t -= BF16_SUBLANES
    return t


def kernel(x, meta_tokens, mix_norm_w, w_in, ssd_conv_w, ssd_conv_b, ssd_dt_bias, ssd_a_log, ssd_d,
           ssd_norm_w, dn_conv_w, dn_dt_bias, dn_a_log, dn_norm_w, w_branch, w_out, ffn_norm_w,
           w_gate_up, w_down, final_norm_w):
    batch, seq, _ = x.shape
    t_pad = PAD + N_META + seq
    nc = t_pad // CHUNK
    assert t_pad % CHUNK == 0 and mix_norm_w.shape[0] == 1

    head = jnp.concatenate([jnp.zeros((PAD, D_MODEL), x.dtype), meta_tokens.astype(x.dtype)], axis=0)
    h_pad = jnp.concatenate([jnp.broadcast_to(head[None], (batch, CHUNK, D_MODEL)), x], axis=1)
    h_pad = h_pad.reshape(batch * t_pad, D_MODEL)

    w = w_in[0]
    o_zs, o_xbc = 0, SSD_D_INNER
    o_dt = o_xbc + SSD_CONV_DIM
    o_qkv = o_dt + SSD_HEADS
    o_a = o_qkv + DN_CONV_DIM
    o_b = o_a + DN_V_HEADS
    o_zd = o_b + DN_V_HEADS
    o_gate = o_zd + DN_VALUE_DIM
    w_main = jnp.concatenate([w[:, o_zs:o_xbc], w[:, o_zd:o_gate], w[:, o_qkv:o_a], w[:, o_xbc:o_dt]],
                             axis=1).astype(BF16)
    w_small = jnp.concatenate([w[:, o_dt:o_qkv], w[:, o_a:o_zd],
                               jnp.zeros((D_MODEL, SMALL_W - SSD_HEADS - 2 * DN_V_HEADS), w.dtype)],
                              axis=1).astype(BF16)
    w_gates = w[:, o_gate:].astype(BF16)

    m_pad = batch * t_pad
    main, small = _inproj(h_pad, mix_norm_w.astype(F32), w_main, w_small,
                          _row_tile(m_pad, INPROJ_ROWS), INPROJ_COLS)

    tri = jnp.asarray(np.tril(np.ones((CHUNK, CHUNK), np.float32)), BF16)
    e_ssd = jnp.asarray(_expand_matrix(COL_DT, SSD_HEAD_DIM, SSD_HEADS), BF16)
    e_dn = jnp.asarray(
        np.concatenate([_expand_matrix(COL_B, CHUNK, DN_V_HEADS) + _expand_matrix(COL_A, CHUNK, DN_V_HEADS),
                        _expand_matrix(COL_B, DN_HEAD, DN_V_HEADS) + _expand_matrix(COL_A, DN_HEAD, DN_V_HEADS)],
                       axis=1), BF16)

    y_s = _ssd(main, small, ssd_conv_w[0].astype(F32), ssd_conv_b.astype(F32),
               _pad_lanes(ssd_dt_bias[0], COL_DT), _pad_lanes(ssd_a_log[0], COL_DT),
               jnp.repeat(ssd_d[0].astype(F32), SSD_HEAD_DIM)[None, :], ssd_norm_w.astype(F32),
               tri, e_ssd, batch, nc)
    y_d = _dn(main, small, dn_conv_w[0].astype(F32), _pad_lanes(dn_dt_bias[0], COL_A),
              _pad_lanes(dn_a_log[0], COL_A), dn_norm_w.astype(F32), tri, e_dn, batch, nc)

    x2d = x.reshape(batch * seq, D_MODEL)
    tm = _row_tile(batch * seq, DENSE_ROWS)
    h1 = _merge(x2d, y_s, y_d, mix_norm_w.astype(F32), w_gates, w_branch[0, 0].astype(BF16),
                w_branch[0, 1].astype(BF16), w_out[0].astype(BF16), tm)
    out = _ffn(h1, ffn_norm_w.astype(F32), w_gate_up[0].astype(BF16), w_down[0].astype(BF16),
               final_norm_w.astype(F32)[None, :], tm)
    return out.reshape(batch, seq, D_MODEL)
```

```python
import numpy as np
import jax
import jax.numpy as jnp
from jax import lax
from jax.experimental import pallas as pl
from jax.experimental.pallas import tpu as pltpu

F32 = jnp.float32
BF16 = jnp.bfloat16

D_MODEL = 1024
CHUNK = 64
N_META = 16
PAD = CHUNK - N_META
NORM_EPS = 1e-6

SSD_D_INNER = 2048
SSD_HEAD_DIM = 64
SSD_HEADS = 32
SSD_GROUPS = 4
SSD_STATE = 128
SSD_CONV = 4
SSD_CONV_DIM = 3072
SSD_GROUP_W = SSD_D_INNER // SSD_GROUPS

DN_QK_HEADS = 8
DN_V_HEADS = 16
DN_HEAD = 128
DN_KEY_DIM = 1024
DN_VALUE_DIM = 2048
DN_CONV = 4
DN_CONV_DIM = 4096

D_FF = 2816
SMALL_W = 128
COL_DT, COL_A, COL_B = 0, 32, 48
MAIN_W = SSD_D_INNER + DN_VALUE_DIM + DN_CONV_DIM + SSD_CONV_DIM
CONV_HALO = 16
CONV_COLS = 256
VMEM_LIMIT = 56 * 1024 * 1024
BF16_SUBLANES = 16
INPROJ_ROWS, INPROJ_COLS = 1376, 1024
DENSE_ROWS = 512


def _silu(v):
    return v * jax.nn.sigmoid(v)


def _dot(a, b):
    return jnp.dot(a, b, preferred_element_type=F32)


def _dot_nt(a, b):
    return lax.dot_general(a, b, (((1,), (1,)), ((), ())), preferred_element_type=F32)


def _split3(v):
    p1 = v.astype(BF16)
    r1 = v - p1.astype(F32)
    p2 = r1.astype(BF16)
    p3 = (r1 - p2.astype(F32)).astype(BF16)
    return p1, p2, p3


def _dot_sel_right(sel3, v):
    return _dot(sel3, jnp.concatenate(_split3(v), axis=0))


def _dot_sel_left(v, sel3):
    return _dot(jnp.concatenate(_split3(v), axis=1), sel3)


def _inproj_kernel(h_ref, nw_ref, w_ref, ws_ref, o_ref, os_ref, u_scr):
    @pl.when(pl.program_id(1) == 0)
    def _():
        x = h_ref[...]
        ms = jnp.mean(x * x, axis=-1, keepdims=True)
        u = (x * lax.rsqrt(ms + NORM_EPS) * nw_ref[...]).astype(BF16)
        u_scr[...] = u
        os_ref[...] = _dot(u, ws_ref[...])

    o_ref[...] = _dot(u_scr[...], w_ref[...]).astype(BF16)


def _inproj(h_pad, norm_w, w_main, w_small, tm, tn):
    m = h_pad.shape[0]
    return pl.pallas_call(
        _inproj_kernel,
        out_shape=(jax.ShapeDtypeStruct((m, MAIN_W), BF16),
                   jax.ShapeDtypeStruct((m, SMALL_W), F32)),
        grid=(m // tm, MAIN_W // tn),
        in_specs=[pl.BlockSpec((tm, D_MODEL), lambda i, j: (i, 0)),
                  pl.BlockSpec((1, D_MODEL), lambda i, j: (0, 0)),
                  pl.BlockSpec((D_MODEL, tn), lambda i, j: (0, j)),
                  pl.BlockSpec((D_MODEL, SMALL_W), lambda i, j: (0, 0))],
        out_specs=(pl.BlockSpec((tm, tn), lambda i, j: (i, j)),
                   pl.BlockSpec((tm, SMALL_W), lambda i, j: (i, 0))),
        scratch_shapes=[pltpu.VMEM((tm, D_MODEL), BF16)],
        compiler_params=pltpu.CompilerParams(
            dimension_semantics=("arbitrary", "arbitrary"), vmem_limit_bytes=VMEM_LIMIT),
        name="inproj",
    )(h_pad, norm_w, w_main, w_small)


def _conv_units(raw_refs, halo_scr, shift_ref, cw_ref, bias_ref, out_scr):
    units = []
    col = 0
    for r in raw_refs:
        for c0 in range(0, r.shape[1], CONV_COLS):
            def unit(r=r, c0=c0, cs=slice(col + c0, col + c0 + CONV_COLS)):
                xe = jnp.concatenate([halo_scr[:, cs], r[:, c0:c0 + CONV_COLS]], axis=0)
                taps = _dot(shift_ref[...], xe)
                acc = cw_ref[SSD_CONV - 1:SSD_CONV, cs] * r[:, c0:c0 + CONV_COLS].astype(F32)
                for k in range(SSD_CONV - 1):
                    acc = acc + cw_ref[k:k + 1, cs] * taps[k * CHUNK:(k + 1) * CHUNK]
                if bias_ref is not None:
                    acc = acc + bias_ref[:, cs]
                out_scr[:, cs] = _silu(acc)
            units.append(unit)
        col += r.shape[1]

    def carry():
        col = 0
        for r in raw_refs:
            halo_scr[:, col:col + r.shape[1]] = r[CHUNK - CONV_HALO:CHUNK, :]
            col += r.shape[1]

    return units, carry


def _interleave(units, stages):
    per = -(-len(units) // len(stages))
    for i, stage in enumerate(stages):
        for unit in units[i * per:(i + 1) * per]:
            unit()
        stage()


def _valid_rows(first):
    row = lax.broadcasted_iota(jnp.int32, (CHUNK, 1), 0)
    return jnp.where(jnp.logical_or(jnp.logical_not(first), row >= PAD), 1.0, 0.0).astype(F32)


def _cum_masks(width):
    li = lax.broadcasted_iota(jnp.int32, (CHUNK, width), 0)
    si = lax.broadcasted_iota(jnp.int32, (CHUNK, width), 1) & (CHUNK - 1)
    return li, si


def _store_block_diag(w_ref, p):
    pb = p.astype(BF16)
    for j in range(4):
        js = slice(j * CHUNK, (j + 1) * CHUNK)
        w_ref[js, js] = pb[:, js]


def _ssd_kernel(zs_ref, xa_ref, xb_ref, sm_ref, cw_ref, cb_ref, dtb_ref, alog_ref, dexp_ref, nw_ref,
                tri_ref, e_ref, shift_ref, y_ref, halo_scr, xbc_scr, s_scr, wbd_scr):
    nb = zs_ref.shape[0]
    first = pl.program_id(0) == 0

    @pl.when(first)
    def _():
        halo_scr[...] = jnp.zeros(halo_scr.shape, BF16)
        s_scr[...] = jnp.zeros(s_scr.shape, F32)
        wbd_scr[...] = jnp.zeros(wbd_scr.shape, BF16)

    units, carries = [], []
    for b in range(nb):
        u, c = _conv_units((xa_ref.at[b], xb_ref.at[b]), halo_scr.at[b], shift_ref, cw_ref, cb_ref,
                           xbc_scr.at[b])
        units += u
        carries.append(c)

    lane = lax.broadcasted_iota(jnp.int32, (1, SMALL_W), 1)
    is_dt = lane < SSD_HEADS
    valid = _valid_rows(first)
    li, si = _cum_masks(SSD_D_INNER)
    st = [dict() for _ in range(nb)]

    def stage_dt():
        for b in range(nb):
            dt = jnp.where(is_dt, jax.nn.softplus(sm_ref[b] + dtb_ref[...]) * valid, 0.0)
            st[b]["dt"] = dt
            st[b]["a"] = dt * jnp.where(is_dt, -jnp.exp(alog_ref[...]), 0.0)

    def stage_cumsum():
        for b in range(nb):
            st[b]["acum"] = _dot_sel_right(tri_ref[...], st[b]["a"])

    def stage_expand():
        for b in range(nb):
            st[b]["ex"] = _dot_sel_left(jnp.concatenate([st[b]["dt"], st[b]["acum"]], axis=0),
                                        e_ref[...])

    def stage_decay():
        for b in range(nb):
            ac_e = st[b]["ex"][CHUNK:2 * CHUNK]
            arow = jnp.sum(jnp.where(li == si, ac_e, 0.0), axis=0, keepdims=True)
            alast = ac_e[CHUNK - 1:CHUNK, :]
            st[b].update(dt_e=st[b]["ex"][0:CHUNK],
                         lmat=jnp.exp(jnp.where(li >= si, ac_e - arow, -jnp.inf)),
                         to_end=jnp.exp(alast - ac_e), from_start=jnp.exp(ac_e),
                         chunk_decay=jnp.exp(alast))

    _interleave(units, [stage_dt, stage_cumsum, stage_expand, stage_decay])
    for c in carries:
        c()

    c_off = SSD_D_INNER + SSD_GROUPS * SSD_STATE
    for g in range(SSD_GROUPS):
        cs = slice(g * SSD_GROUP_W, (g + 1) * SSD_GROUP_W)
        for b in range(nb):
            b_g = xbc_scr[b, :, SSD_D_INNER + g * SSD_STATE:SSD_D_INNER + (g + 1) * SSD_STATE]
            c_g = xbc_scr[b, :, c_off + g * SSD_STATE:c_off + (g + 1) * SSD_STATE]
            x_g = xbc_scr[b, :, cs]
            c_b = c_g.astype(BF16)
            b_b = b_g.astype(BF16)
            cb_t = _dot_nt(c_b, jnp.concatenate([b_b] * 8, axis=0))
            scores = cb_t * st[b]["lmat"][:, cs]
            xdt = x_g * st[b]["dt_e"][:, cs]
            y_parts = []
            for q in range(2):
                qs = slice(q * 4 * CHUNK, (q + 1) * 4 * CHUNK)
                w_bd = wbd_scr.at[(b * SSD_GROUPS + g) * 2 + q]
                _store_block_diag(w_bd, xdt[:, qs])
                y_parts.append(_dot(scores[:, qs].astype(BF16), w_bd[...]))
            y_diag = jnp.concatenate(y_parts, axis=1)
            s_old = s_scr[b, :, cs]
            y_off = _dot(c_b, s_old.astype(BF16)) * st[b]["from_start"][:, cs]
            xw = (xdt * st[b]["to_end"][:, cs]).astype(BF16)
            s_scr[b, :, cs] = s_old * st[b]["chunk_decay"][:, cs] + _dot(b_g.T.astype(BF16), xw)
            y = y_diag + y_off + dexp_ref[:, cs] * x_g
            y = y * _silu(zs_ref[b, :, cs].astype(F32))
            ms = jnp.mean(y * y, axis=-1, keepdims=True)
            y_ref[b, :, cs] = (y * lax.rsqrt(ms + NORM_EPS) * nw_ref[:, cs]).astype(BF16)


def _ssd(main, small, cw, cb, dtb, alog, dexp, nw, tri, e_ssd, shift):
    batch, t_pad, _ = main.shape
    nc = t_pad // CHUNK
    x_off = SSD_D_INNER + DN_VALUE_DIM + DN_CONV_DIM
    const = lambda c: (0, 0)
    return pl.pallas_call(
        _ssd_kernel,
        out_shape=jax.ShapeDtypeStruct((batch, (nc - 1) * CHUNK, SSD_D_INNER), BF16),
        grid=(nc,),
        in_specs=[pl.BlockSpec((batch, CHUNK, SSD_D_INNER), lambda c: (0, c, 0)),
                  pl.BlockSpec((batch, CHUNK, 2048), lambda c: (0, c, x_off // 2048)),
                  pl.BlockSpec((batch, CHUNK, 1024), lambda c: (0, c, (x_off + 2048) // 1024)),
                  pl.BlockSpec((batch, CHUNK, SMALL_W), lambda c: (0, c, 0)),
                  pl.BlockSpec((SSD_CONV, SSD_CONV_DIM), const),
                  pl.BlockSpec((1, SSD_CONV_DIM), const),
                  pl.BlockSpec((1, SMALL_W), const),
                  pl.BlockSpec((1, SMALL_W), const),
                  pl.BlockSpec((1, SSD_D_INNER), const),
                  pl.BlockSpec((1, SSD_D_INNER), const),
                  pl.BlockSpec((CHUNK, 3 * CHUNK), const),
                  pl.BlockSpec((3 * SMALL_W, SSD_D_INNER), const),
                  pl.BlockSpec(shift.shape, const)],
        out_specs=pl.BlockSpec((batch, CHUNK, SSD_D_INNER), lambda c: (0, jnp.maximum(c - 1, 0), 0)),
        scratch_shapes=[pltpu.VMEM((batch, CONV_HALO, SSD_CONV_DIM), BF16),
                        pltpu.VMEM((batch, CHUNK, SSD_CONV_DIM), F32),
                        pltpu.VMEM((batch, SSD_STATE, SSD_D_INNER), F32),
                        pltpu.VMEM((batch * SSD_HEADS // 4, 4 * CHUNK, 4 * CHUNK), BF16)],
        compiler_params=pltpu.CompilerParams(
            dimension_semantics=("arbitrary",), vmem_limit_bytes=VMEM_LIMIT),
        name="ssd_mixer",
    )(main, main, main, small, cw, cb, dtb, alog, dexp, nw, tri, e_ssd, shift)


def _l2norm(v):
    return v * lax.rsqrt(jnp.sum(v * v, axis=-1, keepdims=True) + NORM_EPS)


def _dn_kernel(zd_ref, qkv_ref, sm_ref, cw_ref, dtb_ref, alog_ref, nw_ref, tri_ref, e_ref, shift_ref,
               y_ref, halo_scr, qkv_scr, s_scr, wbd_scr):
    nb = zd_ref.shape[0]
    first = pl.program_id(0) == 0

    @pl.when(first)
    def _():
        halo_scr[...] = jnp.zeros(halo_scr.shape, BF16)
        s_scr[...] = jnp.zeros(s_scr.shape, F32)
        wbd_scr[...] = jnp.zeros(wbd_scr.shape, BF16)

    units, carries = [], []
    for b in range(nb):
        u, c = _conv_units((qkv_ref.at[b],), halo_scr.at[b], shift_ref, cw_ref, None, qkv_scr.at[b])
        units += u
        carries.append(c)

    lane = lax.broadcasted_iota(jnp.int32, (1, SMALL_W), 1)
    is_a = jnp.logical_and(lane >= COL_A, lane < COL_A + DN_V_HEADS)
    is_b = jnp.logical_and(lane >= COL_B, lane < COL_B + DN_V_HEADS)
    valid = _valid_rows(first)
    w64 = DN_V_HEADS * CHUNK
    li, si = _cum_masks(w64)
    gates = [dict() for _ in range(nb)]

    def stage_gate():
        for b in range(nb):
            sm = sm_ref[b]
            gates[b]["beta"] = jnp.where(is_b, jax.nn.sigmoid(sm) * valid, 0.0)
            neg_a = jnp.where(is_a, -jnp.exp(alog_ref[...]), 0.0)
            gates[b]["g"] = neg_a * jax.nn.softplus(sm + dtb_ref[...]) * valid

    def stage_cumsum():
        for b in range(nb):
            gates[b]["gcum"] = _dot_sel_right(tri_ref[...], gates[b]["g"])

    def stage_expand():
        for b in range(nb):
            gates[b]["ex"] = _dot_sel_left(jnp.concatenate([gates[b]["beta"], gates[b]["gcum"]], axis=0),
                                           e_ref[...])

    def stage_decay():
        for b in range(nb):
            ex = gates[b]["ex"]
            gc64 = ex[CHUNK:, 0:w64]
            gc128 = ex[CHUNK:, w64:]
            grow = jnp.sum(jnp.where(li == si, gc64, 0.0), axis=0, keepdims=True)
            glast = gc128[CHUNK - 1:CHUNK, :]
            gates[b].update(beta64=ex[0:CHUNK, 0:w64], beta128=ex[0:CHUNK, w64:],
                            dec=jnp.exp(jnp.where(li >= si, gc64 - grow, -jnp.inf)),
                            from_start=jnp.exp(gc128), to_end=jnp.exp(glast - gc128),
                            chunk_decay=jnp.exp(glast))

    _interleave(units, [stage_gate, stage_cumsum, stage_expand, stage_decay])
    for c in carries:
        c()

    li2, si2 = _cum_masks(2 * CHUNK)
    strict = li2 > si2
    li4, si4 = _cum_masks(4 * CHUNK)
    eye4 = jnp.where(li4 == si4, 1.0, 0.0).astype(F32)

    q_heads, k_heads, a_pairs, qkd_pairs = [], [], [], []
    for b in range(nb):
        for i in range(DN_QK_HEADS):
            q_i = _l2norm(qkv_scr[b, :, i * DN_HEAD:(i + 1) * DN_HEAD]) * (DN_HEAD ** -0.5)
            k_i = _l2norm(qkv_scr[b, :, DN_KEY_DIM + i * DN_HEAD:DN_KEY_DIM + (i + 1) * DN_HEAD])
            k_b = k_i.astype(BF16)
            prod = _dot_nt(jnp.concatenate([q_i.astype(BF16), k_b], axis=0),
                           jnp.concatenate([k_b, k_b], axis=0))
            ps = slice(i * 2 * CHUNK, (i + 1) * 2 * CHUNK)
            d_pair = gates[b]["dec"][:, ps]
            qkd_pairs.append(prod[0:CHUNK] * d_pair)
            a_pairs.append(jnp.where(strict, gates[b]["beta64"][:, ps] * prod[CHUNK:] * d_pair, 0.0))
            q_heads.append(q_i)
            k_heads.append(k_i)

    quads = range(nb * DN_V_HEADS // 4)
    n_mats = [-jnp.concatenate(a_pairs[2 * q:2 * q + 2], axis=1) for q in quads]
    qkds = [jnp.concatenate(qkd_pairs[2 * q:2 * q + 2], axis=1) for q in quads]
    t_mats = [eye4 + n for n in n_mats]

    def times_block_diag(lhs, p_list):
        for q in quads:
            _store_block_diag(wbd_scr.at[q], p_list[q])
        return [_dot(lhs[q].astype(BF16), wbd_scr[q]) for q in quads]

    p_mats = times_block_diag(n_mats, n_mats)
    for _ in range(4):
        boths = times_block_diag([jnp.concatenate([p_mats[q], t_mats[q]], axis=0) for q in quads], p_mats)
        p_mats = [b[0:CHUNK] for b in boths]
        t_mats = [t + b[CHUNK:] for t, b in zip(t_mats, boths)]
    t_mats = [t + d for t, d in zip(t_mats, times_block_diag(t_mats, p_mats))]

    heads = [(b, h) for b in range(nb) for h in range(DN_V_HEADS)]
    hsl = [slice(h * DN_HEAD, (h + 1) * DN_HEAD) for h in range(DN_V_HEADS)]
    uws = []
    for n, (b, h) in enumerate(heads):
        k_i = k_heads[n // 2]
        v_h = qkv_scr[b, :, 2 * DN_KEY_DIM + h * DN_HEAD:2 * DN_KEY_DIM + (h + 1) * DN_HEAD]
        b_h = gates[b]["beta128"][:, hsl[h]]
        rhs = jnp.concatenate([v_h * b_h, k_i * (b_h * gates[b]["from_start"][:, hsl[h]])], axis=1)
        j = h % 4
        t_h = t_mats[n // 4][:, j * CHUNK:(j + 1) * CHUNK].astype(BF16)
        uws.append(_dot(t_h, rhs.astype(BF16)))

    s_olds = [s_scr[n] for n in range(len(heads))]
    ws_qs = [_dot(jnp.concatenate([uws[n][:, DN_HEAD:], q_heads[n // 2] * gates[b]["from_start"][:, hsl[h]]],
                                  axis=0).astype(BF16), s_olds[n].astype(BF16))
             for n, (b, h) in enumerate(heads)]
    v_news = [(uws[n][:, 0:DN_HEAD] - ws_qs[n][0:CHUNK]).astype(BF16) for n in range(len(heads))]
    for n, (b, h) in enumerate(heads):
        k_dec = k_heads[n // 2] * gates[b]["to_end"][:, hsl[h]]
        s_scr[n] = s_olds[n] * gates[b]["chunk_decay"][:, hsl[h]] + _dot(k_dec.T.astype(BF16), v_news[n])
    for n, (b, h) in enumerate(heads):
        j = h % 4
        o_h = ws_qs[n][CHUNK:] + _dot(qkds[n // 4][:, j * CHUNK:(j + 1) * CHUNK].astype(BF16), v_news[n])
        ms = jnp.mean(o_h * o_h, axis=-1, keepdims=True)
        y = o_h * lax.rsqrt(ms + NORM_EPS) * nw_ref[...]
        y_ref[b, :, hsl[h]] = (y * _silu(zd_ref[b, :, hsl[h]].astype(F32))).astype(BF16)


def _dn(main, small, cw, dtb, alog, nw, tri, e_dn, shift):
    batch, t_pad, _ = main.shape
    nc = t_pad // CHUNK
    const = lambda c: (0, 0)
    return pl.pallas_call(
        _dn_kernel,
        out_shape=jax.ShapeDtypeStruct((batch, (nc - 1) * CHUNK, DN_VALUE_DIM), BF16),
        grid=(nc,),
        in_specs=[pl.BlockSpec((batch, CHUNK, DN_VALUE_DIM), lambda c: (0, c, SSD_D_INNER // DN_VALUE_DIM)),
                  pl.BlockSpec((batch, CHUNK, DN_CONV_DIM),
                               lambda c: (0, c, (SSD_D_INNER + DN_VALUE_DIM) // DN_CONV_DIM)),
                  pl.BlockSpec((batch, CHUNK, SMALL_W), lambda c: (0, c, 0)),
                  pl.BlockSpec((DN_CONV, DN_CONV_DIM), const),
                  pl.BlockSpec((1, SMALL_W), const),
                  pl.BlockSpec((1, SMALL_W), const),
                  pl.BlockSpec((1, DN_HEAD), const),
                  pl.BlockSpec((CHUNK, 3 * CHUNK), const),
                  pl.BlockSpec(e_dn.shape, const),
                  pl.BlockSpec(shift.shape, const)],
        out_specs=pl.BlockSpec((batch, CHUNK, DN_VALUE_DIM), lambda c: (0, jnp.maximum(c - 1, 0), 0)),
        scratch_shapes=[pltpu.VMEM((batch, CONV_HALO, DN_CONV_DIM), BF16),
                        pltpu.VMEM((batch, CHUNK, DN_CONV_DIM), F32),
                        pltpu.VMEM((batch * DN_V_HEADS, DN_HEAD, DN_HEAD), F32),
                        pltpu.VMEM((batch * DN_V_HEADS // 4, 4 * CHUNK, 4 * CHUNK), BF16)],
        compiler_params=pltpu.CompilerParams(
            dimension_semantics=("arbitrary",), vmem_limit_bytes=VMEM_LIMIT),
        name="dn_mixer",
    )(main, main, small, cw, dtb, alog, nw, tri, e_dn, shift)


def _rms(v):
    return v * lax.rsqrt(jnp.mean(v * v, axis=-1, keepdims=True) + NORM_EPS)


def _dense_kernel(x_ref, ys_ref, yd_ref, nw_ref, wg_ref, wb0_ref, wb1_ref, wo_ref, fnw_ref, wgu_ref,
                  wd_ref, fw_ref, o_ref):
    x = x_ref[...]
    u = (_rms(x) * nw_ref[...]).astype(BF16)
    gates = jax.nn.sigmoid(_dot(u, wg_ref[...]))
    merged = (gates[:, 0:D_MODEL] * _dot(ys_ref[...], wb0_ref[...])
              + gates[:, D_MODEL:] * _dot(yd_ref[...], wb1_ref[...]))
    h1 = x + _dot(merged.astype(BF16), wo_ref[...])
    u2 = (_rms(h1) * fnw_ref[...]).astype(BF16)
    gu = _dot(u2, wgu_ref[...])
    act = (_silu(gu[:, 0:D_FF]) * gu[:, D_FF:]).astype(BF16)
    h2 = h1 + _dot(act, wd_ref[...])
    o_ref[...] = _rms(h2) * fw_ref[...]


def _dense(x2d, ys, yd, nw, wg, wb0, wb1, wo, fnw, wgu, wd, fw, tm):
    m = x2d.shape[0]
    rows = lambda i: (i, 0)

    def resident(shape):
        return pl.BlockSpec(shape, lambda i: (0, 0), pipeline_mode=pl.Buffered(1))

    return pl.pallas_call(
        _dense_kernel,
        out_shape=jax.ShapeDtypeStruct((m, D_MODEL), F32),
        grid=(m // tm,),
        in_specs=[pl.BlockSpec((tm, D_MODEL), rows),
                  pl.BlockSpec((tm, SSD_D_INNER), rows),
                  pl.BlockSpec((tm, DN_VALUE_DIM), rows),
                  resident((1, D_MODEL)),
                  resident((D_MODEL, 2 * D_MODEL)),
                  resident((SSD_D_INNER, D_MODEL)),
                  resident((DN_VALUE_DIM, D_MODEL)),
                  resident((D_MODEL, D_MODEL)),
                  resident((1, D_MODEL)),
                  resident((D_MODEL, 2 * D_FF)),
                  resident((D_FF, D_MODEL)),
                  resident((1, D_MODEL))],
        out_specs=pl.BlockSpec((tm, D_MODEL), rows),
        compiler_params=pltpu.CompilerParams(
            dimension_semantics=("arbitrary",), vmem_limit_bytes=VMEM_LIMIT),
        name="merge_ffn",
    )(x2d, ys, yd, nw, wg, wb0, wb1, wo, fnw, wgu, wd, fw)


def _pad_lanes(vec, offset):
    out = jnp.zeros((1, SMALL_W), F32)
    return lax.dynamic_update_slice(out, vec.astype(F32)[None, :], (0, offset))


def _expand_matrix(src_rows, group, heads):
    e = np.zeros((SMALL_W, heads * group), np.float32)
    for h in range(heads):
        e[src_rows + h, h * group:(h + 1) * group] = 1.0
    return e


def _shift_matrix():
    s = np.zeros(((SSD_CONV - 1) * CHUNK, CONV_HALO + CHUNK), np.float32)
    for k in range(SSD_CONV - 1):
        for t in range(CHUNK):
            s[k * CHUNK + t, CONV_HALO + t - (SSD_CONV - 1) + k] = 1.0
    return s


def _row_tile(m, target):
    t = min(target, m) // BF16_SUBLANES * BF16_SUBLANES
    while m % t:
        t -= BF16_SUBLANES
    return t


def kernel(x, meta_tokens, mix_norm_w, w_in, ssd_conv_w, ssd_conv_b, ssd_dt_bias, ssd_a_log, ssd_d,
           ssd_norm_w, dn_conv_w, dn_dt_bias, dn_a_log, dn_norm_w, w_branch, w_out, ffn_norm_w,
           w_gate_up, w_down, final_norm_w):
    batch, seq, _ = x.shape
    t_pad = PAD + N_META + seq
    assert t_pad % CHUNK == 0 and mix_norm_w.shape[0] == 1

    head = jnp.concatenate([jnp.zeros((PAD, D_MODEL), x.dtype), meta_tokens.astype(x.dtype)], axis=0)
    h_pad = jnp.concatenate([jnp.broadcast_to(head[None], (batch, CHUNK, D_MODEL)), x], axis=1)
    h_pad = h_pad.reshape(batch * t_pad, D_MODEL)

    w = w_in[0]
    o_zs, o_xbc = 0, SSD_D_INNER
    o_dt = o_xbc + SSD_CONV_DIM
    o_qkv = o_dt + SSD_HEADS
    o_a = o_qkv + DN_CONV_DIM
    o_b = o_a + DN_V_HEADS
    o_zd = o_b + DN_V_HEADS
    o_gate = o_zd + DN_VALUE_DIM
    w_main = jnp.concatenate([w[:, o_zs:o_xbc], w[:, o_zd:o_gate], w[:, o_qkv:o_a], w[:, o_xbc:o_dt]],
                             axis=1).astype(BF16)
    w_small = jnp.concatenate([w[:, o_dt:o_qkv], w[:, o_a:o_zd],
                               jnp.zeros((D_MODEL, SMALL_W - SSD_HEADS - 2 * DN_V_HEADS), w.dtype)],
                              axis=1).astype(BF16)
    w_gates = w[:, o_gate:].astype(BF16)

    m_pad = batch * t_pad
    main, small = _inproj(h_pad, mix_norm_w.astype(F32), w_main, w_small,
                          _row_tile(m_pad, INPROJ_ROWS), INPROJ_COLS)

    tri = jnp.asarray(np.tile(np.tril(np.ones((CHUNK, CHUNK), np.float32)), (1, 3)), BF16)
    shift = jnp.asarray(_shift_matrix(), BF16)
    e_ssd = jnp.asarray(np.tile(_expand_matrix(COL_DT, SSD_HEAD_DIM, SSD_HEADS), (3, 1)), BF16)
    e_dn = jnp.asarray(np.tile(
        np.concatenate([_expand_matrix(COL_B, CHUNK, DN_V_HEADS) + _expand_matrix(COL_A, CHUNK, DN_V_HEADS),
                        _expand_matrix(COL_B, DN_HEAD, DN_V_HEADS) + _expand_matrix(COL_A, DN_HEAD, DN_V_HEADS)],
                       axis=1), (3, 1)), BF16)

    main = main.reshape(batch, t_pad, MAIN_W)
    small = small.reshape(batch, t_pad, SMALL_W)
    y_s = _ssd(main, small, ssd_conv_w[0].astype(F32), ssd_conv_b.astype(F32),
               _pad_lanes(ssd_dt_bias[0], COL_DT), _pad_lanes(ssd_a_log[0], COL_DT),
               jnp.repeat(ssd_d[0].astype(F32), SSD_HEAD_DIM)[None, :], ssd_norm_w.astype(F32),
               tri, e_ssd, shift)
    y_d = _dn(main, small, dn_conv_w[0].astype(F32), _pad_lanes(dn_dt_bias[0], COL_A),
              _pad_lanes(dn_a_log[0], COL_A), dn_norm_w.astype(F32), tri, e_dn, shift)

    x2d = x.reshape(batch * seq, D_MODEL)
    y_s = y_s.reshape(batch * seq, SSD_D_INNER)
    y_d = y_d.reshape(batch * seq, DN_VALUE_DIM)
    out = _dense(x2d, y_s, y_d, mix_norm_w.astype(F32), w_gates, w_branch[0, 0].astype(BF16),
                 w_branch[0, 1].astype(BF16), w_out[0].astype(BF16), ffn_norm_w.astype(F32),
                 w_gate_up[0].astype(BF16), w_down[0].astype(BF16), final_norm_w.astype(F32)[None, :],
                 _row_tile(batch * seq, DENSE_ROWS))
    return out.reshape(batch, seq, D_MODEL)
```

```python
import numpy as np
import jax
import jax.numpy as jnp
from jax import lax
from jax.experimental import pallas as pl
from jax.experimental.pallas import tpu as pltpu

F32 = jnp.float32
BF16 = jnp.bfloat16

D_MODEL = 1024
CHUNK = 64
N_META = 16
PAD = CHUNK - N_META
NORM_EPS = 1e-6

SSD_D_INNER = 2048
SSD_HEAD_DIM = 64
SSD_HEADS = 32
SSD_GROUPS = 4
SSD_STATE = 128
SSD_CONV = 4
SSD_CONV_DIM = 3072
SSD_GROUP_W = SSD_D_INNER // SSD_GROUPS

DN_QK_HEADS = 8
DN_V_HEADS = 16
DN_HEAD = 128
DN_KEY_DIM = 1024
DN_VALUE_DIM = 2048
DN_CONV = 4
DN_CONV_DIM = 4096

D_FF = 2816
SMALL_W = 128
COL_DT, COL_A, COL_B = 0, 32, 48
MAIN_W = SSD_D_INNER + DN_VALUE_DIM + DN_CONV_DIM + SSD_CONV_DIM
CONV_HALO = 16
CONV_COLS = 256
VMEM_LIMIT = 56 * 1024 * 1024
BF16_SUBLANES = 16
INPROJ_ROWS, INPROJ_COLS = 1376, 1024
DENSE_ROWS = 512


def _silu(v):
    return v * jax.nn.sigmoid(v)


def _dot(a, b):
    return jnp.dot(a, b, preferred_element_type=F32)


def _dot_nt(a, b):
    return lax.dot_general(a, b, (((1,), (1,)), ((), ())), preferred_element_type=F32)


def _split3(v):
    p1 = v.astype(BF16)
    r1 = v - p1.astype(F32)
    p2 = r1.astype(BF16)
    p3 = (r1 - p2.astype(F32)).astype(BF16)
    return p1, p2, p3


def _dot_sel_right(sel3, v):
    return _dot(sel3, jnp.concatenate(_split3(v), axis=0))


def _dot_sel_left(v, sel3):
    return _dot(jnp.concatenate(_split3(v), axis=1), sel3)


def _inproj_kernel(h_ref, nw_ref, w_ref, ws_ref, o_ref, os_ref, u_scr):
    @pl.when(pl.program_id(2) == 0)
    def _():
        x = h_ref[...]
        ms = jnp.mean(x * x, axis=-1, keepdims=True)
        u = (x * lax.rsqrt(ms + NORM_EPS) * nw_ref[...]).astype(BF16)
        u_scr[...] = u
        os_ref[...] = _dot(u, ws_ref[...])

    o_ref[...] = _dot(u_scr[...], w_ref[...]).astype(BF16)


def _inproj(h_pad, norm_w, w_main, w_small, tm, tn):
    batch, t_pad, _ = h_pad.shape
    return pl.pallas_call(
        _inproj_kernel,
        out_shape=(jax.ShapeDtypeStruct((batch, t_pad, MAIN_W), BF16),
                   jax.ShapeDtypeStruct((batch, t_pad, SMALL_W), F32)),
        grid=(batch, t_pad // tm, MAIN_W // tn),
        in_specs=[pl.BlockSpec((None, tm, D_MODEL), lambda b, i, j: (b, i, 0)),
                  pl.BlockSpec((1, D_MODEL), lambda b, i, j: (0, 0)),
                  pl.BlockSpec((D_MODEL, tn), lambda b, i, j: (0, j)),
                  pl.BlockSpec((D_MODEL, SMALL_W), lambda b, i, j: (0, 0))],
        out_specs=(pl.BlockSpec((None, tm, tn), lambda b, i, j: (b, i, j)),
                   pl.BlockSpec((None, tm, SMALL_W), lambda b, i, j: (b, i, 0))),
        scratch_shapes=[pltpu.VMEM((tm, D_MODEL), BF16)],
        compiler_params=pltpu.CompilerParams(
            dimension_semantics=("arbitrary", "arbitrary", "arbitrary"), vmem_limit_bytes=VMEM_LIMIT),
        name="inproj",
    )(h_pad, norm_w, w_main, w_small)


def _conv_units(raw_refs, halo_scr, shift_ref, cw_ref, bias_ref, out_scr):
    units = []
    col = 0
    for r in raw_refs:
        for c0 in range(0, r.shape[1], CONV_COLS):
            def unit(r=r, c0=c0, cs=slice(col + c0, col + c0 + CONV_COLS)):
                xe = jnp.concatenate([halo_scr[:, cs], r[:, c0:c0 + CONV_COLS]], axis=0)
                taps = _dot(shift_ref[...], xe)
                acc = cw_ref[SSD_CONV - 1:SSD_CONV, cs] * r[:, c0:c0 + CONV_COLS].astype(F32)
                for k in range(SSD_CONV - 1):
                    acc = acc + cw_ref[k:k + 1, cs] * taps[k * CHUNK:(k + 1) * CHUNK]
                if bias_ref is not None:
                    acc = acc + bias_ref[:, cs]
                out_scr[:, cs] = _silu(acc)
            units.append(unit)
        col += r.shape[1]

    def carry():
        col = 0
        for r in raw_refs:
            halo_scr[:, col:col + r.shape[1]] = r[CHUNK - CONV_HALO:CHUNK, :]
            col += r.shape[1]

    return units, carry


def _interleave(units, stages):
    per = -(-len(units) // len(stages))
    for i, stage in enumerate(stages):
        for unit in units[i * per:(i + 1) * per]:
            unit()
        stage()


def _valid_rows(first):
    row = lax.broadcasted_iota(jnp.int32, (CHUNK, 1), 0)
    return jnp.where(jnp.logical_or(jnp.logical_not(first), row >= PAD), 1.0, 0.0).astype(F32)


def _cum_masks(width):
    li = lax.broadcasted_iota(jnp.int32, (CHUNK, width), 0)
    si = lax.broadcasted_iota(jnp.int32, (CHUNK, width), 1) & (CHUNK - 1)
    return li, si


def _store_block_diag(w_ref, p):
    pb = p.astype(BF16)
    for j in range(4):
        js = slice(j * CHUNK, (j + 1) * CHUNK)
        w_ref[js, js] = pb[:, js]


def _ssd_stages(zs_ref, xa_ref, xb_ref, sm_ref, cw_ref, cb_ref, dtb_ref, alog_ref, dexp_ref, nw_ref,
                tri_ref, e_ref, shift_ref, y_ref, halo_scr, xbc_scr, s_scr, wbd_scr, valid):
    nb = zs_ref.shape[0]
    units, carries = [], []
    for b in range(nb):
        u, c = _conv_units((xa_ref.at[b], xb_ref.at[b]), halo_scr.at[b], shift_ref, cw_ref, cb_ref,
                           xbc_scr.at[b])
        units += u
        carries.append(c)

    lane = lax.broadcasted_iota(jnp.int32, (1, SMALL_W), 1)
    is_dt = lane < SSD_HEADS
    li, si = _cum_masks(SSD_D_INNER)
    st = [dict() for _ in range(nb)]

    def stage_dt():
        for b in range(nb):
            dt = jnp.where(is_dt, jax.nn.softplus(sm_ref[b] + dtb_ref[...]) * valid, 0.0)
            st[b]["dt"] = dt
            st[b]["a"] = dt * jnp.where(is_dt, -jnp.exp(alog_ref[...]), 0.0)

    def stage_cumsum():
        for b in range(nb):
            st[b]["acum"] = _dot_sel_right(tri_ref[...], st[b]["a"])

    def stage_expand():
        for b in range(nb):
            st[b]["ex"] = _dot_sel_left(jnp.concatenate([st[b]["dt"], st[b]["acum"]], axis=0),
                                        e_ref[...])

    def stage_decay():
        for b in range(nb):
            ac_e = st[b]["ex"][CHUNK:2 * CHUNK]
            arow = jnp.sum(jnp.where(li == si, ac_e, 0.0), axis=0, keepdims=True)
            alast = ac_e[CHUNK - 1:CHUNK, :]
            st[b].update(dt_e=st[b]["ex"][0:CHUNK],
                         lmat=jnp.exp(jnp.where(li >= si, ac_e - arow, -jnp.inf)),
                         to_end=jnp.exp(alast - ac_e), from_start=jnp.exp(ac_e),
                         chunk_decay=jnp.exp(alast))

    c_off = SSD_D_INNER + SSD_GROUPS * SSD_STATE

    def group_unit(g, b):
        cs = slice(g * SSD_GROUP_W, (g + 1) * SSD_GROUP_W)
        b_g = xbc_scr[b, :, SSD_D_INNER + g * SSD_STATE:SSD_D_INNER + (g + 1) * SSD_STATE]
        c_g = xbc_scr[b, :, c_off + g * SSD_STATE:c_off + (g + 1) * SSD_STATE]
        x_g = xbc_scr[b, :, cs]
        c_b = c_g.astype(BF16)
        b_b = b_g.astype(BF16)
        cb_t = _dot_nt(c_b, jnp.concatenate([b_b] * 8, axis=0))
        scores = cb_t * st[b]["lmat"][:, cs]
        xdt = x_g * st[b]["dt_e"][:, cs]
        y_parts = []
        for q in range(2):
            qs = slice(q * 4 * CHUNK, (q + 1) * 4 * CHUNK)
            w_bd = wbd_scr.at[(b * SSD_GROUPS + g) * 2 + q]
            _store_block_diag(w_bd, xdt[:, qs])
            y_parts.append(_dot(scores[:, qs].astype(BF16), w_bd[...]))
        y_diag = jnp.concatenate(y_parts, axis=1)
        s_old = s_scr[b, :, cs]
        y_off = _dot(c_b, s_old.astype(BF16)) * st[b]["from_start"][:, cs]
        xw = (xdt * st[b]["to_end"][:, cs]).astype(BF16)
        s_scr[b, :, cs] = s_old * st[b]["chunk_decay"][:, cs] + _dot(b_g.T.astype(BF16), xw)
        y = y_diag + y_off + dexp_ref[:, cs] * x_g
        y = y * _silu(zs_ref[b, :, cs].astype(F32))
        ms = jnp.mean(y * y, axis=-1, keepdims=True)
        y_ref[b, :, cs] = (y * lax.rsqrt(ms + NORM_EPS) * nw_ref[:, cs]).astype(BF16)

    groups = [(lambda g=g, b=b: group_unit(g, b)) for g in range(SSD_GROUPS) for b in range(nb)]
    return units, [stage_dt, stage_cumsum, stage_expand, stage_decay], carries, groups


def _l2norm(v):
    return v * lax.rsqrt(jnp.sum(v * v, axis=-1, keepdims=True) + NORM_EPS)


def _dn_stages(zd_ref, qkv_ref, sm_ref, cw_ref, dtb_ref, alog_ref, nw_ref, tri_ref, e_ref, shift_ref,
               y_ref, halo_scr, qkv_scr, s_scr, wbd_scr, valid):
    nb = zd_ref.shape[0]
    units, carries = [], []
    for b in range(nb):
        u, c = _conv_units((qkv_ref.at[b],), halo_scr.at[b], shift_ref, cw_ref, None, qkv_scr.at[b])
        units += u
        carries.append(c)

    lane = lax.broadcasted_iota(jnp.int32, (1, SMALL_W), 1)
    is_a = jnp.logical_and(lane >= COL_A, lane < COL_A + DN_V_HEADS)
    is_b = jnp.logical_and(lane >= COL_B, lane < COL_B + DN_V_HEADS)
    w64 = DN_V_HEADS * CHUNK
    li, si = _cum_masks(w64)
    gates = [dict() for _ in range(nb)]

    def stage_gate():
        for b in range(nb):
            sm = sm_ref[b]
            gates[b]["beta"] = jnp.where(is_b, jax.nn.sigmoid(sm) * valid, 0.0)
            neg_a = jnp.where(is_a, -jnp.exp(alog_ref[...]), 0.0)
            gates[b]["g"] = neg_a * jax.nn.softplus(sm + dtb_ref[...]) * valid

    def stage_cumsum():
        for b in range(nb):
            gates[b]["gcum"] = _dot_sel_right(tri_ref[...], gates[b]["g"])

    def stage_expand():
        for b in range(nb):
            gates[b]["ex"] = _dot_sel_left(jnp.concatenate([gates[b]["beta"], gates[b]["gcum"]], axis=0),
                                           e_ref[...])

    def stage_decay():
        for b in range(nb):
            ex = gates[b]["ex"]
            gc64 = ex[CHUNK:, 0:w64]
            gc128 = ex[CHUNK:, w64:]
            grow = jnp.sum(jnp.where(li == si, gc64, 0.0), axis=0, keepdims=True)
            glast = gc128[CHUNK - 1:CHUNK, :]
            gates[b].update(beta64=ex[0:CHUNK, 0:w64], beta128=ex[0:CHUNK, w64:],
                            dec=jnp.exp(jnp.where(li >= si, gc64 - grow, -jnp.inf)),
                            from_start=jnp.exp(gc128), to_end=jnp.exp(glast - gc128),
                            chunk_decay=jnp.exp(glast))

    quads = range(nb * DN_V_HEADS // 4)
    heads = [(b, h) for b in range(nb) for h in range(DN_V_HEADS)]
    hsl = [slice(h * DN_HEAD, (h + 1) * DN_HEAD) for h in range(DN_V_HEADS)]
    v = dict(q=[], k=[], a=[], qkd=[])

    def stage_gram():
        li2, si2 = _cum_masks(2 * CHUNK)
        strict = li2 > si2
        for b in range(nb):
            for i in range(DN_QK_HEADS):
                q_i = _l2norm(qkv_scr[b, :, i * DN_HEAD:(i + 1) * DN_HEAD]) * (DN_HEAD ** -0.5)
                k_i = _l2norm(qkv_scr[b, :, DN_KEY_DIM + i * DN_HEAD:DN_KEY_DIM + (i + 1) * DN_HEAD])
                k_b = k_i.astype(BF16)
                prod = _dot_nt(jnp.concatenate([q_i.astype(BF16), k_b], axis=0),
                               jnp.concatenate([k_b, k_b], axis=0))
                ps = slice(i * 2 * CHUNK, (i + 1) * 2 * CHUNK)
                d_pair = gates[b]["dec"][:, ps]
                v["qkd"].append(prod[0:CHUNK] * d_pair)
                v["a"].append(jnp.where(strict, gates[b]["beta64"][:, ps] * prod[CHUNK:] * d_pair, 0.0))
                v["q"].append(q_i)
                v["k"].append(k_i)

    def times_block_diag(lhs, p_list):
        for q in quads:
            _store_block_diag(wbd_scr.at[q], p_list[q])
        return [_dot(lhs[q].astype(BF16), wbd_scr[q]) for q in quads]

    def inverse_first():
        li4, si4 = _cum_masks(4 * CHUNK)
        eye4 = jnp.where(li4 == si4, 1.0, 0.0).astype(F32)
        n_mats = [-jnp.concatenate(v["a"][2 * q:2 * q + 2], axis=1) for q in quads]
        v["qkds"] = [jnp.concatenate(v["qkd"][2 * q:2 * q + 2], axis=1) for q in quads]
        v["t"] = [eye4 + n for n in n_mats]
        v["p"] = times_block_diag(n_mats, n_mats)

    def inverse_round():
        boths = times_block_diag([jnp.concatenate([v["p"][q], v["t"][q]], axis=0) for q in quads], v["p"])
        v["p"] = [b[0:CHUNK] for b in boths]
        v["t"] = [t + b[CHUNK:] for t, b in zip(v["t"], boths)]

    def inverse_last():
        v["t"] = [t + d for t, d in zip(v["t"], times_block_diag(v["t"], v["p"]))]

    def stage_uw():
        v["uw"] = []
        for n, (b, h) in enumerate(heads):
            k_i = v["k"][n // 2]
            v_h = qkv_scr[b, :, 2 * DN_KEY_DIM + h * DN_HEAD:2 * DN_KEY_DIM + (h + 1) * DN_HEAD]
            b_h = gates[b]["beta128"][:, hsl[h]]
            rhs = jnp.concatenate([v_h * b_h, k_i * (b_h * gates[b]["from_start"][:, hsl[h]])], axis=1)
            j = h % 4
            t_h = v["t"][n // 4][:, j * CHUNK:(j + 1) * CHUNK].astype(BF16)
            v["uw"].append(_dot(t_h, rhs.astype(BF16)))

    def stage_state_read():
        v["s_old"] = [s_scr[n] for n in range(len(heads))]
        v["ws_qs"] = [_dot(jnp.concatenate([v["uw"][n][:, DN_HEAD:],
                                            v["q"][n // 2] * gates[b]["from_start"][:, hsl[h]]],
                                           axis=0).astype(BF16), v["s_old"][n].astype(BF16))
                      for n, (b, h) in enumerate(heads)]
        v["v_new"] = [(v["uw"][n][:, 0:DN_HEAD] - v["ws_qs"][n][0:CHUNK]).astype(BF16)
                      for n in range(len(heads))]

    def stage_state_write():
        for n, (b, h) in enumerate(heads):
            k_dec = v["k"][n // 2] * gates[b]["to_end"][:, hsl[h]]
            s_scr[n] = (v["s_old"][n] * gates[b]["chunk_decay"][:, hsl[h]]
                        + _dot(k_dec.T.astype(BF16), v["v_new"][n]))

    def stage_out():
        for n, (b, h) in enumerate(heads):
            j = h % 4
            o_h = v["ws_qs"][n][CHUNK:] + _dot(v["qkds"][n // 4][:, j * CHUNK:(j + 1) * CHUNK].astype(BF16),
                                               v["v_new"][n])
            ms = jnp.mean(o_h * o_h, axis=-1, keepdims=True)
            y = o_h * lax.rsqrt(ms + NORM_EPS) * nw_ref[...]
            y_ref[b, :, hsl[h]] = (y * _silu(zd_ref[b, :, hsl[h]].astype(F32))).astype(BF16)

    return dict(units=units, gates=[stage_gate, stage_cumsum, stage_expand, stage_decay], carries=carries,
                gram=stage_gram, inverse=[inverse_first] + [inverse_round] * 4 + [inverse_last],
                tail=[stage_uw, stage_state_read, stage_state_write, stage_out])


def _mixer_kernel(zs_ref, xa_ref, xb_ref, zd_ref, qkv_ref, sm_ref,
                  scw_ref, scb_ref, sdtb_ref, salog_ref, sdexp_ref, snw_ref, se_ref,
                  dcw_ref, ddtb_ref, dalog_ref, dnw_ref, de_ref, tri_ref, shift_ref,
                  ys_ref, yd_ref,
                  s_halo, s_xbc, s_state, s_wbd, d_halo, d_qkv, d_state, d_wbd):
    first = pl.program_id(0) == 0

    @pl.when(first)
    def _():
        for scr in (s_halo, s_state, s_wbd, d_halo, d_state, d_wbd):
            scr[...] = jnp.zeros(scr.shape, scr.dtype)

    valid = _valid_rows(first)
    s_units, s_gates, s_carries, s_groups = _ssd_stages(
        zs_ref, xa_ref, xb_ref, sm_ref, scw_ref, scb_ref, sdtb_ref, salog_ref, sdexp_ref, snw_ref,
        tri_ref, se_ref, shift_ref, ys_ref, s_halo, s_xbc, s_state, s_wbd, valid)
    dn = _dn_stages(zd_ref, qkv_ref, sm_ref, dcw_ref, ddtb_ref, dalog_ref, dnw_ref, tri_ref, de_ref,
                    shift_ref, yd_ref, d_halo, d_qkv, d_state, d_wbd, valid)

    _interleave(dn["units"] + s_units, [g for pair in zip(dn["gates"], s_gates) for g in pair])
    for c in dn["carries"] + s_carries:
        c()
    dn["gram"]()
    fill = list(s_groups)
    for stage in dn["inverse"] + dn["tail"]:
        stage()
        if fill:
            fill.pop(0)()
    for unit in fill:
        unit()


def _mixers(main, small, ssd_params, dn_params, tri, shift):
    batch, t_pad, _ = main.shape
    nc = t_pad // CHUNK
    x_off = SSD_D_INNER + DN_VALUE_DIM + DN_CONV_DIM
    const = lambda c: (0, 0)
    chunk = lambda width, col: pl.BlockSpec((batch, CHUNK, width), lambda c: (0, c, col))
    whole = lambda arr: pl.BlockSpec(arr.shape, const)
    out_spec = pl.BlockSpec((batch, CHUNK, SSD_D_INNER), lambda c: (0, jnp.maximum(c - 1, 0), 0))
    seq = (nc - 1) * CHUNK
    return pl.pallas_call(
        _mixer_kernel,
        out_shape=(jax.ShapeDtypeStruct((batch, seq, SSD_D_INNER), BF16),
                   jax.ShapeDtypeStruct((batch, seq, DN_VALUE_DIM), BF16)),
        grid=(nc,),
        in_specs=[chunk(SSD_D_INNER, 0),
                  chunk(2048, x_off // 2048),
                  chunk(1024, (x_off + 2048) // 1024),
                  chunk(DN_VALUE_DIM, SSD_D_INNER // DN_VALUE_DIM),
                  chunk(DN_CONV_DIM, (SSD_D_INNER + DN_VALUE_DIM) // DN_CONV_DIM),
                  chunk(SMALL_W, 0)]
                 + [whole(a) for a in ssd_params] + [whole(a) for a in dn_params] + [whole(tri), whole(shift)],
        out_specs=(out_spec, out_spec),
        scratch_shapes=[pltpu.VMEM((batch, CONV_HALO, SSD_CONV_DIM), BF16),
                        pltpu.VMEM((batch, CHUNK, SSD_CONV_DIM), F32),
                        pltpu.VMEM((batch, SSD_STATE, SSD_D_INNER), F32),
                        pltpu.VMEM((batch * SSD_HEADS // 4, 4 * CHUNK, 4 * CHUNK), BF16),
                        pltpu.VMEM((batch, CONV_HALO, DN_CONV_DIM), BF16),
                        pltpu.VMEM((batch, CHUNK, DN_CONV_DIM), F32),
                        pltpu.VMEM((batch * DN_V_HEADS, DN_HEAD, DN_HEAD), F32),
                        pltpu.VMEM((batch * DN_V_HEADS // 4, 4 * CHUNK, 4 * CHUNK), BF16)],
        compiler_params=pltpu.CompilerParams(
            dimension_semantics=("arbitrary",), vmem_limit_bytes=VMEM_LIMIT),
        name="mixers",
    )(main, main, main, main, main, small, *ssd_params, *dn_params, tri, shift)


def _rms(v):
    return v * lax.rsqrt(jnp.mean(v * v, axis=-1, keepdims=True) + NORM_EPS)


def _dense_kernel(x_ref, ys_ref, yd_ref, nw_ref, wg_ref, wb0_ref, wb1_ref, wo_ref, fnw_ref, wgu_ref,
                  wd_ref, fw_ref, o_ref):
    x = x_ref[...]
    u = (_rms(x) * nw_ref[...]).astype(BF16)
    gates = jax.nn.sigmoid(_dot(u, wg_ref[...]))
    merged = (gates[:, 0:D_MODEL] * _dot(ys_ref[...], wb0_ref[...])
              + gates[:, D_MODEL:] * _dot(yd_ref[...], wb1_ref[...]))
    h1 = x + _dot(merged.astype(BF16), wo_ref[...])
    u2 = (_rms(h1) * fnw_ref[...]).astype(BF16)
    gu = _dot(u2, wgu_ref[...])
    act = (_silu(gu[:, 0:D_FF]) * gu[:, D_FF:]).astype(BF16)
    h2 = h1 + _dot(act, wd_ref[...])
    o_ref[...] = _rms(h2) * fw_ref[...]


def _dense(x2d, ys, yd, nw, wg, wb0, wb1, wo, fnw, wgu, wd, fw, tm):
    m = x2d.shape[0]
    rows = lambda i: (i, 0)

    def resident(shape):
        return pl.BlockSpec(shape, lambda i: (0, 0), pipeline_mode=pl.Buffered(1))

    return pl.pallas_call(
        _dense_kernel,
        out_shape=jax.ShapeDtypeStruct((m, D_MODEL), F32),
        grid=(m // tm,),
        in_specs=[pl.BlockSpec((tm, D_MODEL), rows),
                  pl.BlockSpec((tm, SSD_D_INNER), rows),
                  pl.BlockSpec((tm, DN_VALUE_DIM), rows),
                  resident((1, D_MODEL)),
                  resident((D_MODEL, 2 * D_MODEL)),
                  resident((SSD_D_INNER, D_MODEL)),
                  resident((DN_VALUE_DIM, D_MODEL)),
                  resident((D_MODEL, D_MODEL)),
                  resident((1, D_MODEL)),
                  resident((D_MODEL, 2 * D_FF)),
                  resident((D_FF, D_MODEL)),
                  resident((1, D_MODEL))],
        out_specs=pl.BlockSpec((tm, D_MODEL), rows),
        compiler_params=pltpu.CompilerParams(
            dimension_semantics=("arbitrary",), vmem_limit_bytes=VMEM_LIMIT),
        name="merge_ffn",
    )(x2d, ys, yd, nw, wg, wb0, wb1, wo, fnw, wgu, wd, fw)


def _pad_lanes(vec, offset):
    out = jnp.zeros((1, SMALL_W), F32)
    return lax.dynamic_update_slice(out, vec.astype(F32)[None, :], (0, offset))


def _expand_matrix(src_rows, group, heads):
    e = np.zeros((SMALL_W, heads * group), np.float32)
    for h in range(heads):
        e[src_rows + h, h * group:(h + 1) * group] = 1.0
    return e


def _shift_matrix():
    s = np.zeros(((SSD_CONV - 1) * CHUNK, CONV_HALO + CHUNK), np.float32)
    for k in range(SSD_CONV - 1):
        for t in range(CHUNK):
            s[k * CHUNK + t, CONV_HALO + t - (SSD_CONV - 1) + k] = 1.0
    return s


def _row_tile(m, target):
    t = min(target, m) // BF16_SUBLANES * BF16_SUBLANES
    while m % t:
        t -= BF16_SUBLANES
    return t


def kernel(x, meta_tokens, mix_norm_w, w_in, ssd_conv_w, ssd_conv_b, ssd_dt_bias, ssd_a_log, ssd_d,
           ssd_norm_w, dn_conv_w, dn_dt_bias, dn_a_log, dn_norm_w, w_branch, w_out, ffn_norm_w,
           w_gate_up, w_down, final_norm_w):
    batch, seq, _ = x.shape
    t_pad = PAD + N_META + seq
    assert t_pad % CHUNK == 0 and mix_norm_w.shape[0] == 1

    head = jnp.concatenate([jnp.zeros((PAD, D_MODEL), x.dtype), meta_tokens.astype(x.dtype)], axis=0)
    h_pad = jnp.concatenate([jnp.broadcast_to(head[None], (batch, CHUNK, D_MODEL)), x], axis=1)

    w = w_in[0]
    o_zs, o_xbc = 0, SSD_D_INNER
    o_dt = o_xbc + SSD_CONV_DIM
    o_qkv = o_dt + SSD_HEADS
    o_a = o_qkv + DN_CONV_DIM
    o_b = o_a + DN_V_HEADS
    o_zd = o_b + DN_V_HEADS
    o_gate = o_zd + DN_VALUE_DIM
    w_main = jnp.concatenate([w[:, o_zs:o_xbc], w[:, o_zd:o_gate], w[:, o_qkv:o_a], w[:, o_xbc:o_dt]],
                             axis=1).astype(BF16)
    w_small = jnp.concatenate([w[:, o_dt:o_qkv], w[:, o_a:o_zd],
                               jnp.zeros((D_MODEL, SMALL_W - SSD_HEADS - 2 * DN_V_HEADS), w.dtype)],
                              axis=1).astype(BF16)
    w_gates = w[:, o_gate:].astype(BF16)

    main, small = _inproj(h_pad, mix_norm_w.astype(F32), w_main, w_small,
                          _row_tile(t_pad, INPROJ_ROWS), INPROJ_COLS)

    tri = jnp.asarray(np.tile(np.tril(np.ones((CHUNK, CHUNK), np.float32)), (1, 3)), BF16)
    shift = jnp.asarray(_shift_matrix(), BF16)
    e_ssd = jnp.asarray(np.tile(_expand_matrix(COL_DT, SSD_HEAD_DIM, SSD_HEADS), (3, 1)), BF16)
    e_dn = jnp.asarray(np.tile(
        np.concatenate([_expand_matrix(COL_B, CHUNK, DN_V_HEADS) + _expand_matrix(COL_A, CHUNK, DN_V_HEADS),
                        _expand_matrix(COL_B, DN_HEAD, DN_V_HEADS) + _expand_matrix(COL_A, DN_HEAD, DN_V_HEADS)],
                       axis=1), (3, 1)), BF16)

    ssd_params = (ssd_conv_w[0].astype(F32), ssd_conv_b.astype(F32), _pad_lanes(ssd_dt_bias[0], COL_DT),
                  _pad_lanes(ssd_a_log[0], COL_DT), jnp.repeat(ssd_d[0].astype(F32), SSD_HEAD_DIM)[None, :],
                  ssd_norm_w.astype(F32), e_ssd)
    dn_params = (dn_conv_w[0].astype(F32), _pad_lanes(dn_dt_bias[0], COL_A), _pad_lanes(dn_a_log[0], COL_A),
                 dn_norm_w.astype(F32), e_dn)
    y_s, y_d = _mixers(main, small, ssd_params, dn_params, tri, shift)

    x2d = x.reshape(batch * seq, D_MODEL)
    y_s = y_s.reshape(batch * seq, SSD_D_INNER)
    y_d = y_d.reshape(batch * seq, DN_VALUE_DIM)
    out = _dense(x2d, y_s, y_d, mix_norm_w.astype(F32), w_gates, w_branch[0, 0].astype(BF16),
                 w_branch[0, 1].astype(BF16), w_out[0].astype(BF16), ffn_norm_w.astype(F32),
                 w_gate_up[0].astype(BF16), w_down[0].astype(BF16), final_norm_w.astype(F32)[None, :],
                 _row_tile(batch * seq, DENSE_ROWS))
    return out.reshape(batch, seq, D_MODEL)
```

```python
import numpy as np
import jax
import jax.numpy as jnp
from jax import lax
from jax.experimental import pallas as pl
from jax.experimental.pallas import tpu as pltpu

F32 = jnp.float32
BF16 = jnp.bfloat16

D_MODEL = 1024
CHUNK = 64
N_META = 16
PAD = CHUNK - N_META
NORM_EPS = 1e-6

SSD_D_INNER = 2048
SSD_HEAD_DIM = 64
SSD_HEADS = 32
SSD_GROUPS = 4
SSD_STATE = 128
SSD_CONV = 4
SSD_CONV_DIM = 3072
SSD_GROUP_W = SSD_D_INNER // SSD_GROUPS

DN_QK_HEADS = 8
DN_V_HEADS = 16
DN_HEAD = 128
DN_KEY_DIM = 1024
DN_VALUE_DIM = 2048
DN_CONV = 4
DN_CONV_DIM = 4096

D_FF = 2816
SMALL_W = 128
COL_DT, COL_A, COL_B = 0, 32, 48
MAIN_W = SSD_D_INNER + DN_VALUE_DIM + DN_CONV_DIM + SSD_CONV_DIM
CONV_HALO = 16
CONV_COLS = 256
VMEM_LIMIT = 56 * 1024 * 1024
BF16_SUBLANES = 16
INPROJ_ROWS, INPROJ_COLS = 1376, 1408
DENSE_ROWS = 512


def _silu(v):
    return v * jax.nn.sigmoid(v)


def _dot(a, b):
    return jnp.dot(a, b, preferred_element_type=F32)


def _dot_nt(a, b):
    return lax.dot_general(a, b, (((1,), (1,)), ((), ())), preferred_element_type=F32)


def _split3(v):
    p1 = v.astype(BF16)
    r1 = v - p1.astype(F32)
    p2 = r1.astype(BF16)
    p3 = (r1 - p2.astype(F32)).astype(BF16)
    return p1, p2, p3


def _dot_sel_right(sel3, v):
    return _dot(sel3, jnp.concatenate(_split3(v), axis=0))


def _dot_sel_left(v, sel3):
    return _dot(jnp.concatenate(_split3(v), axis=1), sel3)


def _inproj_kernel(h_ref, nw_ref, w_ref, ws_ref, o_ref, os_ref, u_scr):
    @pl.when(pl.program_id(1) == 0)
    def _():
        x = h_ref[...]
        ms = jnp.mean(x * x, axis=-1, keepdims=True)
        u = (x * lax.rsqrt(ms + NORM_EPS) * nw_ref[...]).astype(BF16)
        u_scr[...] = u
        os_ref[...] = _dot(u, ws_ref[...])

    o_ref[...] = _dot(u_scr[...], w_ref[...]).astype(BF16)


def _inproj(h_pad, norm_w, w_main, w_small, tm, tn):
    m = h_pad.shape[0]
    return pl.pallas_call(
        _inproj_kernel,
        out_shape=(jax.ShapeDtypeStruct((m, MAIN_W), BF16),
                   jax.ShapeDtypeStruct((m, SMALL_W), F32)),
        grid=(m // tm, MAIN_W // tn),
        in_specs=[pl.BlockSpec((tm, D_MODEL), lambda i, j: (i, 0)),
                  pl.BlockSpec((1, D_MODEL), lambda i, j: (0, 0)),
                  pl.BlockSpec((D_MODEL, tn), lambda i, j: (0, j)),
                  pl.BlockSpec((D_MODEL, SMALL_W), lambda i, j: (0, 0))],
        out_specs=(pl.BlockSpec((tm, tn), lambda i, j: (i, j)),
                   pl.BlockSpec((tm, SMALL_W), lambda i, j: (i, 0))),
        scratch_shapes=[pltpu.VMEM((tm, D_MODEL), BF16)],
        compiler_params=pltpu.CompilerParams(
            dimension_semantics=("arbitrary", "arbitrary"), vmem_limit_bytes=VMEM_LIMIT),
        name="inproj",
    )(h_pad, norm_w, w_main, w_small)


def _conv_units(raw_refs, halo_scr, shift_ref, cw_ref, bias_ref, out_scr):
    units = []
    col = 0
    for r in raw_refs:
        for c0 in range(0, r.shape[1], CONV_COLS):
            def unit(r=r, c0=c0, cs=slice(col + c0, col + c0 + CONV_COLS)):
                xe = jnp.concatenate([halo_scr[:, cs], r[:, c0:c0 + CONV_COLS]], axis=0)
                taps = _dot(shift_ref[...], xe)
                acc = cw_ref[SSD_CONV - 1:SSD_CONV, cs] * r[:, c0:c0 + CONV_COLS].astype(F32)
                for k in range(SSD_CONV - 1):
                    acc = acc + cw_ref[k:k + 1, cs] * taps[k * CHUNK:(k + 1) * CHUNK]
                if bias_ref is not None:
                    acc = acc + bias_ref[:, cs]
                out_scr[:, cs] = _silu(acc)
            units.append(unit)
        col += r.shape[1]

    def carry():
        col = 0
        for r in raw_refs:
            halo_scr[:, col:col + r.shape[1]] = r[CHUNK - CONV_HALO:CHUNK, :]
            col += r.shape[1]

    return units, carry


def _interleave(units, stages):
    per = -(-len(units) // len(stages))
    for i, stage in enumerate(stages):
        for unit in units[i * per:(i + 1) * per]:
            unit()
        stage()


def _valid_rows(first):
    row = lax.broadcasted_iota(jnp.int32, (CHUNK, 1), 0)
    return jnp.where(jnp.logical_or(jnp.logical_not(first), row >= PAD), 1.0, 0.0).astype(F32)


def _cum_masks(width):
    li = lax.broadcasted_iota(jnp.int32, (CHUNK, width), 0)
    si = lax.broadcasted_iota(jnp.int32, (CHUNK, width), 1) & (CHUNK - 1)
    return li, si


def _store_block_diag(w_ref, p):
    pb = p.astype(BF16)
    for j in range(p.shape[1] // CHUNK):
        js = slice(j * CHUNK, (j + 1) * CHUNK)
        w_ref[js, js] = pb[:, js]


def _ssd_kernel(zs_ref, xa_ref, xb_ref, sm_ref, cw_ref, cb_ref, dtb_ref, alog_ref, dexp_ref, nw_ref,
                tri_ref, e_ref, shift_ref, y_ref, halo_scr, xbc_scr, s_scr, wbd_scr):
    nb = zs_ref.shape[0]
    first = pl.program_id(0) == 0

    @pl.when(first)
    def _():
        halo_scr[...] = jnp.zeros(halo_scr.shape, BF16)
        s_scr[...] = jnp.zeros(s_scr.shape, F32)
        wbd_scr[...] = jnp.zeros(wbd_scr.shape, BF16)

    units, carries = [], []
    for b in range(nb):
        u, c = _conv_units((xa_ref.at[b], xb_ref.at[b]), halo_scr.at[b], shift_ref, cw_ref, cb_ref,
                           xbc_scr.at[b])
        units += u
        carries.append(c)

    lane = lax.broadcasted_iota(jnp.int32, (1, SMALL_W), 1)
    is_dt = lane < SSD_HEADS
    valid = _valid_rows(first)
    li, si = _cum_masks(SSD_D_INNER)
    st = [dict() for _ in range(nb)]

    def stage_dt():
        for b in range(nb):
            dt = jnp.where(is_dt, jax.nn.softplus(sm_ref[b] + dtb_ref[...]) * valid, 0.0)
            st[b]["dt"] = dt
            st[b]["a"] = dt * jnp.where(is_dt, -jnp.exp(alog_ref[...]), 0.0)

    def stage_cumsum():
        for b in range(nb):
            st[b]["acum"] = _dot_sel_right(tri_ref[...], st[b]["a"])

    def stage_expand():
        ex = _dot_sel_left(jnp.concatenate([t for b in range(nb) for t in (st[b]["dt"], st[b]["acum"])], axis=0),
                           e_ref[...])
        for b in range(nb):
            st[b]["ex"] = ex[2 * b * CHUNK:2 * (b + 1) * CHUNK]

    def stage_decay():
        for b in range(nb):
            ac_e = st[b]["ex"][CHUNK:2 * CHUNK]
            arow = jnp.sum(jnp.where(li == si, ac_e, 0.0), axis=0, keepdims=True)
            alast = ac_e[CHUNK - 1:CHUNK, :]
            st[b].update(dt_e=st[b]["ex"][0:CHUNK],
                         lmat=jnp.exp(jnp.where(li >= si, ac_e - arow, -jnp.inf)),
                         to_end=jnp.exp(alast - ac_e), from_start=jnp.exp(ac_e),
                         chunk_decay=jnp.exp(alast))

    _interleave(units, [stage_dt, stage_cumsum, stage_expand, stage_decay])
    for c in carries:
        c()

    c_off = SSD_D_INNER + SSD_GROUPS * SSD_STATE
    for g in range(SSD_GROUPS):
        cs = slice(g * SSD_GROUP_W, (g + 1) * SSD_GROUP_W)
        for b in range(nb):
            b_g = xbc_scr[b, :, SSD_D_INNER + g * SSD_STATE:SSD_D_INNER + (g + 1) * SSD_STATE]
            c_g = xbc_scr[b, :, c_off + g * SSD_STATE:c_off + (g + 1) * SSD_STATE]
            x_g = xbc_scr[b, :, cs]
            c_b = c_g.astype(BF16)
            b_b = b_g.astype(BF16)
            cb_t = _dot_nt(c_b, jnp.concatenate([b_b] * 8, axis=0))
            scores = cb_t * st[b]["lmat"][:, cs]
            xdt = x_g * st[b]["dt_e"][:, cs]
            y_parts = []
            for q in range(2):
                qs = slice(q * 4 * CHUNK, (q + 1) * 4 * CHUNK)
                w_bd = wbd_scr.at[(b * SSD_GROUPS + g) * 2 + q]
                _store_block_diag(w_bd, xdt[:, qs])
                y_parts.append(_dot(scores[:, qs].astype(BF16), w_bd[...]))
            y_diag = jnp.concatenate(y_parts, axis=1)
            s_old = s_scr[b, :, cs]
            y_off = _dot(c_b, s_old.astype(BF16)) * st[b]["from_start"][:, cs]
            xw = (xdt * st[b]["to_end"][:, cs]).astype(BF16)
            s_scr[b, :, cs] = s_old * st[b]["chunk_decay"][:, cs] + _dot(b_g.T.astype(BF16), xw)
            y = y_diag + y_off + dexp_ref[:, cs] * x_g
            y = y * _silu(zs_ref[b, :, cs].astype(F32))
            ms = jnp.mean(y * y, axis=-1, keepdims=True)
            y_ref[b, :, cs] = (y * lax.rsqrt(ms + NORM_EPS) * nw_ref[:, cs]).astype(BF16)


def _ssd(main, small, cw, cb, dtb, alog, dexp, nw, tri, e_ssd, shift):
    batch, t_pad, _ = main.shape
    nc = t_pad // CHUNK
    x_off = SSD_D_INNER + DN_VALUE_DIM + DN_CONV_DIM
    const = lambda c: (0, 0)
    return pl.pallas_call(
        _ssd_kernel,
        out_shape=jax.ShapeDtypeStruct((batch, (nc - 1) * CHUNK, SSD_D_INNER), BF16),
        grid=(nc,),
        in_specs=[pl.BlockSpec((batch, CHUNK, SSD_D_INNER), lambda c: (0, c, 0)),
                  pl.BlockSpec((batch, CHUNK, 2048), lambda c: (0, c, x_off // 2048)),
                  pl.BlockSpec((batch, CHUNK, 1024), lambda c: (0, c, (x_off + 2048) // 1024)),
                  pl.BlockSpec((batch, CHUNK, SMALL_W), lambda c: (0, c, 0)),
                  pl.BlockSpec((SSD_CONV, SSD_CONV_DIM), const),
                  pl.BlockSpec((1, SSD_CONV_DIM), const),
                  pl.BlockSpec((1, SMALL_W), const),
                  pl.BlockSpec((1, SMALL_W), const),
                  pl.BlockSpec((1, SSD_D_INNER), const),
                  pl.BlockSpec((1, SSD_D_INNER), const),
                  pl.BlockSpec((CHUNK, 3 * CHUNK), const),
                  pl.BlockSpec((3 * SMALL_W, SSD_D_INNER), const),
                  pl.BlockSpec(shift.shape, const)],
        out_specs=pl.BlockSpec((batch, CHUNK, SSD_D_INNER), lambda c: (0, jnp.maximum(c - 1, 0), 0)),
        scratch_shapes=[pltpu.VMEM((batch, CONV_HALO, SSD_CONV_DIM), BF16),
                        pltpu.VMEM((batch, CHUNK, SSD_CONV_DIM), F32),
                        pltpu.VMEM((batch, SSD_STATE, SSD_D_INNER), F32),
                        pltpu.VMEM((batch * SSD_HEADS // 4, 4 * CHUNK, 4 * CHUNK), BF16)],
        compiler_params=pltpu.CompilerParams(
            dimension_semantics=("arbitrary",), vmem_limit_bytes=VMEM_LIMIT),
        name="ssd_mixer",
    )(main, main, main, small, cw, cb, dtb, alog, dexp, nw, tri, e_ssd, shift)


def _l2norm(v):
    return v * lax.rsqrt(jnp.sum(v * v, axis=-1, keepdims=True) + NORM_EPS)


def _dn_kernel(zd_ref, qkv_ref, sm_ref, cw_ref, dtb_ref, alog_ref, nw_ref, tri_ref, e_ref, shift_ref,
               y_ref, halo_scr, qkv_scr, s_scr, wbd_scr):
    nb = zd_ref.shape[0]
    first = pl.program_id(0) == 0

    @pl.when(first)
    def _():
        halo_scr[...] = jnp.zeros(halo_scr.shape, BF16)
        s_scr[...] = jnp.zeros(s_scr.shape, F32)
        wbd_scr[...] = jnp.zeros(wbd_scr.shape, BF16)

    units, carries = [], []
    for b in range(nb):
        u, c = _conv_units((qkv_ref.at[b],), halo_scr.at[b], shift_ref, cw_ref, None, qkv_scr.at[b])
        units += u
        carries.append(c)

    lane = lax.broadcasted_iota(jnp.int32, (1, SMALL_W), 1)
    is_a = jnp.logical_and(lane >= COL_A, lane < COL_A + DN_V_HEADS)
    is_b = jnp.logical_and(lane >= COL_B, lane < COL_B + DN_V_HEADS)
    valid = _valid_rows(first)
    w64 = DN_V_HEADS * CHUNK
    li, si = _cum_masks(w64)
    gates = [dict() for _ in range(nb)]

    def stage_gate():
        for b in range(nb):
            sm = sm_ref[b]
            gates[b]["beta"] = jnp.where(is_b, jax.nn.sigmoid(sm) * valid, 0.0)
            neg_a = jnp.where(is_a, -jnp.exp(alog_ref[...]), 0.0)
            gates[b]["g"] = neg_a * jax.nn.softplus(sm + dtb_ref[...]) * valid

    def stage_cumsum():
        for b in range(nb):
            gates[b]["gcum"] = _dot_sel_right(tri_ref[...], gates[b]["g"])

    def stage_expand():
        ex = _dot_sel_left(jnp.concatenate([t for b in range(nb) for t in (gates[b]["beta"], gates[b]["gcum"])],
                                           axis=0), e_ref[...])
        for b in range(nb):
            gates[b]["ex"] = ex[2 * b * CHUNK:2 * (b + 1) * CHUNK]

    def stage_decay():
        for b in range(nb):
            ex = gates[b]["ex"]
            gc64 = ex[CHUNK:, 0:w64]
            gc128 = ex[CHUNK:, w64:]
            grow = jnp.sum(jnp.where(li == si, gc64, 0.0), axis=0, keepdims=True)
            glast = gc128[CHUNK - 1:CHUNK, :]
            gates[b].update(beta64=ex[0:CHUNK, 0:w64], beta128=ex[0:CHUNK, w64:],
                            dec=jnp.exp(jnp.where(li >= si, gc64 - grow, -jnp.inf)),
                            from_start=jnp.exp(gc128), to_end=jnp.exp(glast - gc128),
                            chunk_decay=jnp.exp(glast))

    _interleave(units, [stage_gate, stage_cumsum, stage_expand, stage_decay])
    for c in carries:
        c()

    li2, si2 = _cum_masks(2 * CHUNK)
    strict = li2 > si2
    eye2 = jnp.where(li2 == si2, 1.0, 0.0).astype(F32)

    q_heads, k_heads, n_mats, qkds = [], [], [], []
    for b in range(nb):
        for i in range(DN_QK_HEADS):
            q_i = _l2norm(qkv_scr[b, :, i * DN_HEAD:(i + 1) * DN_HEAD]) * (DN_HEAD ** -0.5)
            k_i = _l2norm(qkv_scr[b, :, DN_KEY_DIM + i * DN_HEAD:DN_KEY_DIM + (i + 1) * DN_HEAD])
            k_b = k_i.astype(BF16)
            prod = _dot_nt(jnp.concatenate([q_i.astype(BF16), k_b], axis=0),
                           jnp.concatenate([k_b, k_b], axis=0))
            ps = slice(i * 2 * CHUNK, (i + 1) * 2 * CHUNK)
            d_pair = gates[b]["dec"][:, ps]
            qkds.append(prod[0:CHUNK] * d_pair)
            n_mats.append(jnp.where(strict, -(gates[b]["beta64"][:, ps] * prod[CHUNK:] * d_pair), 0.0))
            q_heads.append(q_i)
            k_heads.append(k_i)

    pairs = range(nb * DN_QK_HEADS)
    t_mats = [eye2 + n for n in n_mats]

    def times_block_diag(lhs, p_list):
        for q in pairs:
            _store_block_diag(wbd_scr.at[q], p_list[q])
        return [_dot(lhs[q].astype(BF16), wbd_scr[q]) for q in pairs]

    p_mats = times_block_diag(n_mats, n_mats)
    for _ in range(4):
        boths = times_block_diag([jnp.concatenate([p_mats[q], t_mats[q]], axis=0) for q in pairs], p_mats)
        p_mats = [b[0:CHUNK] for b in boths]
        t_mats = [t + b[CHUNK:] for t, b in zip(t_mats, boths)]
    t_mats = [t + d for t, d in zip(t_mats, times_block_diag(t_mats, p_mats))]

    heads = [(b, h) for b in range(nb) for h in range(DN_V_HEADS)]
    hsl = [slice(h * DN_HEAD, (h + 1) * DN_HEAD) for h in range(DN_V_HEADS)]
    uws = []
    for n, (b, h) in enumerate(heads):
        k_i = k_heads[n // 2]
        v_h = qkv_scr[b, :, 2 * DN_KEY_DIM + h * DN_HEAD:2 * DN_KEY_DIM + (h + 1) * DN_HEAD]
        b_h = gates[b]["beta128"][:, hsl[h]]
        rhs = jnp.concatenate([v_h * b_h, k_i * (b_h * gates[b]["from_start"][:, hsl[h]])], axis=1)
        j = h % 2
        t_h = t_mats[n // 2][:, j * CHUNK:(j + 1) * CHUNK].astype(BF16)
        uws.append(_dot(t_h, rhs.astype(BF16)))

    s_olds = [s_scr[n] for n in range(len(heads))]
    ws_qs = [_dot(jnp.concatenate([uws[n][:, DN_HEAD:], q_heads[n // 2] * gates[b]["from_start"][:, hsl[h]]],
                                  axis=0).astype(BF16), s_olds[n].astype(BF16))
             for n, (b, h) in enumerate(heads)]
    v_news = [(uws[n][:, 0:DN_HEAD] - ws_qs[n][0:CHUNK]).astype(BF16) for n in range(len(heads))]
    for n, (b, h) in enumerate(heads):
        k_dec = k_heads[n // 2] * gates[b]["to_end"][:, hsl[h]]
        s_scr[n] = s_olds[n] * gates[b]["chunk_decay"][:, hsl[h]] + _dot(k_dec.T.astype(BF16), v_news[n])
        j = h % 2
        o_h = ws_qs[n][CHUNK:] + _dot(qkds[n // 2][:, j * CHUNK:(j + 1) * CHUNK].astype(BF16), v_news[n])
        ms = jnp.mean(o_h * o_h, axis=-1, keepdims=True)
        y = o_h * lax.rsqrt(ms + NORM_EPS) * nw_ref[...]
        y_ref[b, :, hsl[h]] = (y * _silu(zd_ref[b, :, hsl[h]].astype(F32))).astype(BF16)


def _dn(main, small, cw, dtb, alog, nw, tri, e_dn, shift):
    batch, t_pad, _ = main.shape
    nc = t_pad // CHUNK
    const = lambda c: (0, 0)
    return pl.pallas_call(
        _dn_kernel,
        out_shape=jax.ShapeDtypeStruct((batch, (nc - 1) * CHUNK, DN_VALUE_DIM), BF16),
        grid=(nc,),
        in_specs=[pl.BlockSpec((batch, CHUNK, DN_VALUE_DIM), lambda c: (0, c, SSD_D_INNER // DN_VALUE_DIM)),
                  pl.BlockSpec((batch, CHUNK, DN_CONV_DIM),
                               lambda c: (0, c, (SSD_D_INNER + DN_VALUE_DIM) // DN_CONV_DIM)),
                  pl.BlockSpec((batch, CHUNK, SMALL_W), lambda c: (0, c, 0)),
                  pl.BlockSpec((DN_CONV, DN_CONV_DIM), const),
                  pl.BlockSpec((1, SMALL_W), const),
                  pl.BlockSpec((1, SMALL_W), const),
                  pl.BlockSpec((1, DN_HEAD), const),
                  pl.BlockSpec((CHUNK, 3 * CHUNK), const),
                  pl.BlockSpec(e_dn.shape, const),
                  pl.BlockSpec(shift.shape, const)],
        out_specs=pl.BlockSpec((batch, CHUNK, DN_VALUE_DIM), lambda c: (0, jnp.maximum(c - 1, 0), 0)),
        scratch_shapes=[pltpu.VMEM((batch, CONV_HALO, DN_CONV_DIM), BF16),
                        pltpu.VMEM((batch, CHUNK, DN_CONV_DIM), F32),
                        pltpu.VMEM((batch * DN_V_HEADS, DN_HEAD, DN_HEAD), F32),
                        pltpu.VMEM((batch * DN_QK_HEADS, 2 * CHUNK, 2 * CHUNK), BF16)],
        compiler_params=pltpu.CompilerParams(
            dimension_semantics=("arbitrary",), vmem_limit_bytes=VMEM_LIMIT),
        name="dn_mixer",
    )(main, main, small, cw, dtb, alog, nw, tri, e_dn, shift)


def _rms(v):
    return v * lax.rsqrt(jnp.mean(v * v, axis=-1, keepdims=True) + NORM_EPS)


def _dense_kernel(x_ref, ys_ref, yd_ref, nw_ref, wg_ref, wb0_ref, wb1_ref, wo_ref, fnw_ref, wgu_ref,
                  wd_ref, fw_ref, o_ref):
    x = x_ref[...]
    u = (_rms(x) * nw_ref[...]).astype(BF16)
    gates = jax.nn.sigmoid(_dot(u, wg_ref[...]))
    merged = (gates[:, 0:D_MODEL] * _dot(ys_ref[...], wb0_ref[...])
              + gates[:, D_MODEL:] * _dot(yd_ref[...], wb1_ref[...]))
    h1 = x + _dot(merged.astype(BF16), wo_ref[...])
    u2 = (_rms(h1) * fnw_ref[...]).astype(BF16)
    gu = _dot(u2, wgu_ref[...])
    act = (_silu(gu[:, 0:D_FF]) * gu[:, D_FF:]).astype(BF16)
    h2 = h1 + _dot(act, wd_ref[...])
    o_ref[...] = _rms(h2) * fw_ref[...]


def _dense(x2d, ys, yd, nw, wg, wb0, wb1, wo, fnw, wgu, wd, fw, tm):
    m = x2d.shape[0]
    rows = lambda i: (i, 0)

    def resident(shape):
        return pl.BlockSpec(shape, lambda i: (0, 0), pipeline_mode=pl.Buffered(1))

    return pl.pallas_call(
        _dense_kernel,
        out_shape=jax.ShapeDtypeStruct((m, D_MODEL), F32),
        grid=(m // tm,),
        in_specs=[pl.BlockSpec((tm, D_MODEL), rows),
                  pl.BlockSpec((tm, SSD_D_INNER), rows),
                  pl.BlockSpec((tm, DN_VALUE_DIM), rows),
                  resident((1, D_MODEL)),
                  resident((D_MODEL, 2 * D_MODEL)),
                  resident((SSD_D_INNER, D_MODEL)),
                  resident((DN_VALUE_DIM, D_MODEL)),
                  resident((D_MODEL, D_MODEL)),
                  resident((1, D_MODEL)),
                  resident((D_MODEL, 2 * D_FF)),
                  resident((D_FF, D_MODEL)),
                  resident((1, D_MODEL))],
        out_specs=pl.BlockSpec((tm, D_MODEL), rows),
        compiler_params=pltpu.CompilerParams(
            dimension_semantics=("arbitrary",), vmem_limit_bytes=VMEM_LIMIT),
        name="merge_ffn",
    )(x2d, ys, yd, nw, wg, wb0, wb1, wo, fnw, wgu, wd, fw)


def _pad_lanes(vec, offset):
    out = jnp.zeros((1, SMALL_W), F32)
    return lax.dynamic_update_slice(out, vec.astype(F32)[None, :], (0, offset))


def _expand_matrix(src_rows, group, heads):
    e = np.zeros((SMALL_W, heads * group), np.float32)
    for h in range(heads):
        e[src_rows + h, h * group:(h + 1) * group] = 1.0
    return e


def _shift_matrix():
    s = np.zeros(((SSD_CONV - 1) * CHUNK, CONV_HALO + CHUNK), np.float32)
    for k in range(SSD_CONV - 1):
        for t in range(CHUNK):
            s[k * CHUNK + t, CONV_HALO + t - (SSD_CONV - 1) + k] = 1.0
    return s


def _row_tile(m, target):
    t = min(target, m) // BF16_SUBLANES * BF16_SUBLANES
    while m % t:
        t -= BF16_SUBLANES
    return t


def kernel(x, meta_tokens, mix_norm_w, w_in, ssd_conv_w, ssd_conv_b, ssd_dt_bias, ssd_a_log, ssd_d,
           ssd_norm_w, dn_conv_w, dn_dt_bias, dn_a_log, dn_norm_w, w_branch, w_out, ffn_norm_w,
           w_gate_up, w_down, final_norm_w):
    batch, seq, _ = x.shape
    t_pad = PAD + N_META + seq
    assert t_pad % CHUNK == 0 and mix_norm_w.shape[0] == 1

    head = jnp.concatenate([jnp.zeros((PAD, D_MODEL), x.dtype), meta_tokens.astype(x.dtype)], axis=0)
    h_pad = jnp.concatenate([jnp.broadcast_to(head[None], (batch, CHUNK, D_MODEL)), x], axis=1)
    h_pad = h_pad.reshape(batch * t_pad, D_MODEL)

    w = w_in[0]
    o_zs, o_xbc = 0, SSD_D_INNER
    o_dt = o_xbc + SSD_CONV_DIM
    o_qkv = o_dt + SSD_HEADS
    o_a = o_qkv + DN_CONV_DIM
    o_b = o_a + DN_V_HEADS
    o_zd = o_b + DN_V_HEADS
    o_gate = o_zd + DN_VALUE_DIM
    w_main = jnp.concatenate([w[:, o_zs:o_xbc], w[:, o_zd:o_gate], w[:, o_qkv:o_a], w[:, o_xbc:o_dt]],
                             axis=1).astype(BF16)
    w_small = jnp.concatenate([w[:, o_dt:o_qkv], w[:, o_a:o_zd],
                               jnp.zeros((D_MODEL, SMALL_W - SSD_HEADS - 2 * DN_V_HEADS), w.dtype)],
                              axis=1).astype(BF16)
    w_gates = w[:, o_gate:].astype(BF16)

    m_pad = batch * t_pad
    main, small = _inproj(h_pad, mix_norm_w.astype(F32), w_main, w_small,
                          _row_tile(m_pad, INPROJ_ROWS), INPROJ_COLS)

    tri = jnp.asarray(np.tile(np.tril(np.ones((CHUNK, CHUNK), np.float32)), (1, 3)), BF16)
    shift = jnp.asarray(_shift_matrix(), BF16)
    e_ssd = jnp.asarray(np.tile(_expand_matrix(COL_DT, SSD_HEAD_DIM, SSD_HEADS), (3, 1)), BF16)
    e_dn = jnp.asarray(np.tile(
        np.concatenate([_expand_matrix(COL_B, CHUNK, DN_V_HEADS) + _expand_matrix(COL_A, CHUNK, DN_V_HEADS),
                        _expand_matrix(COL_B, DN_HEAD, DN_V_HEADS) + _expand_matrix(COL_A, DN_HEAD, DN_V_HEADS)],
                       axis=1), (3, 1)), BF16)

    main = main.reshape(batch, t_pad, MAIN_W)
    small = small.reshape(batch, t_pad, SMALL_W)
    y_s = _ssd(main, small, ssd_conv_w[0].astype(F32), ssd_conv_b.astype(F32),
               _pad_lanes(ssd_dt_bias[0], COL_DT), _pad_lanes(ssd_a_log[0], COL_DT),
               jnp.repeat(ssd_d[0].astype(F32), SSD_HEAD_DIM)[None, :], ssd_norm_w.astype(F32),
               tri, e_ssd, shift)
    y_d = _dn(main, small, dn_conv_w[0].astype(F32), _pad_lanes(dn_dt_bias[0], COL_A),
              _pad_lanes(dn_a_log[0], COL_A), dn_norm_w.astype(F32), tri, e_dn, shift)

    x2d = x.reshape(batch * seq, D_MODEL)
    y_s = y_s.reshape(batch * seq, SSD_D_INNER)
    y_d = y_d.reshape(batch * seq, DN_VALUE_DIM)
    out = _dense(x2d, y_s, y_d, mix_norm_w.astype(F32), w_gates, w_branch[0, 0].astype(BF16),
                 w_branch[0, 1].astype(BF16), w_out[0].astype(BF16), ffn_norm_w.astype(F32),
                 w_gate_up[0].astype(BF16), w_down[0].astype(BF16), final_norm_w.astype(F32)[None, :],
                 _row_tile(batch * seq, DENSE_ROWS))
    return out.reshape(batch, seq, D_MODEL)
```

```python
import numpy as np
import jax
import jax.numpy as jnp
from jax import lax
from jax.experimental import pallas as pl
from jax.experimental.pallas import tpu as pltpu

F32 = jnp.float32
BF16 = jnp.bfloat16

D_MODEL = 1024
CHUNK = 64
N_META = 16
PAD = CHUNK - N_META
NORM_EPS = 1e-6

SSD_D_INNER = 2048
SSD_HEAD_DIM = 64
SSD_HEADS = 32
SSD_GROUPS = 4
SSD_STATE = 128
SSD_CONV = 4
SSD_CONV_DIM = 3072
SSD_GROUP_W = SSD_D_INNER // SSD_GROUPS

DN_QK_HEADS = 8
DN_V_HEADS = 16
DN_HEAD = 128
DN_KEY_DIM = 1024
DN_VALUE_DIM = 2048
DN_CONV = 4
DN_CONV_DIM = 4096

D_FF = 2816
SMALL_W = 128
COL_DT, COL_A, COL_B = 0, 32, 48
MAIN_W = SSD_D_INNER + DN_VALUE_DIM + DN_CONV_DIM + SSD_CONV_DIM
CONV_HALO = 16
CONV_COLS = 256
VMEM_LIMIT = 56 * 1024 * 1024
BF16_SUBLANES = 16
INPROJ_ROWS, INPROJ_COLS = 1376, 1024
DENSE_ROWS = 512


def _silu(v):
    return v * jax.nn.sigmoid(v)


def _dot(a, b):
    return jnp.dot(a, b, preferred_element_type=F32)


def _dot_nt(a, b):
    return lax.dot_general(a, b, (((1,), (1,)), ((), ())), preferred_element_type=F32)


def _split3(v):
    p1 = v.astype(BF16)
    r1 = v - p1.astype(F32)
    p2 = r1.astype(BF16)
    p3 = (r1 - p2.astype(F32)).astype(BF16)
    return p1, p2, p3


def _dot_sel_right(sel3, v):
    return _dot(sel3, jnp.concatenate(_split3(v), axis=0))


def _dot_sel_left(v, sel3):
    return _dot(jnp.concatenate(_split3(v), axis=1), sel3)


def _inproj_kernel(h_ref, nw_ref, w_ref, ws_ref, o_ref, os_ref, u_scr):
    @pl.when(pl.program_id(1) == 0)
    def _():
        x = h_ref[...]
        ms = jnp.mean(x * x, axis=-1, keepdims=True)
        u = (x * lax.rsqrt(ms + NORM_EPS) * nw_ref[...]).astype(BF16)
        u_scr[...] = u
        os_ref[...] = _dot(u, ws_ref[...])

    o_ref[...] = _dot(u_scr[...], w_ref[...]).astype(BF16)


def _inproj(h_pad, norm_w, w_main, w_small, tm, tn):
    m = h_pad.shape[0]
    return pl.pallas_call(
        _inproj_kernel,
        out_shape=(jax.ShapeDtypeStruct((m, MAIN_W), BF16),
                   jax.ShapeDtypeStruct((m, SMALL_W), F32)),
        grid=(m // tm, MAIN_W // tn),
        in_specs=[pl.BlockSpec((tm, D_MODEL), lambda i, j: (i, 0)),
                  pl.BlockSpec((1, D_MODEL), lambda i, j: (0, 0)),
                  pl.BlockSpec((D_MODEL, tn), lambda i, j: (0, j)),
                  pl.BlockSpec((D_MODEL, SMALL_W), lambda i, j: (0, 0))],
        out_specs=(pl.BlockSpec((tm, tn), lambda i, j: (i, j)),
                   pl.BlockSpec((tm, SMALL_W), lambda i, j: (i, 0))),
        scratch_shapes=[pltpu.VMEM((tm, D_MODEL), BF16)],
        compiler_params=pltpu.CompilerParams(
            dimension_semantics=("arbitrary", "arbitrary"), vmem_limit_bytes=VMEM_LIMIT),
        name="inproj",
    )(h_pad, norm_w, w_main, w_small)


def _conv_units(raw_refs, halo_scr, shift_ref, cw_ref, bias_ref, out_scr):
    units = []
    col = 0
    for r in raw_refs:
        for c0 in range(0, r.shape[1], CONV_COLS):
            def unit(r=r, c0=c0, cs=slice(col + c0, col + c0 + CONV_COLS)):
                xe = jnp.concatenate([halo_scr[:, cs], r[:, c0:c0 + CONV_COLS]], axis=0)
                taps = _dot(shift_ref[...], xe)
                acc = cw_ref[SSD_CONV - 1:SSD_CONV, cs] * r[:, c0:c0 + CONV_COLS].astype(F32)
                for k in range(SSD_CONV - 1):
                    acc = acc + cw_ref[k:k + 1, cs] * taps[k * CHUNK:(k + 1) * CHUNK]
                if bias_ref is not None:
                    acc = acc + bias_ref[:, cs]
                out_scr[:, cs] = _silu(acc)
            units.append(unit)
        col += r.shape[1]

    def carry():
        col = 0
        for r in raw_refs:
            halo_scr[:, col:col + r.shape[1]] = r[CHUNK - CONV_HALO:CHUNK, :]
            col += r.shape[1]

    return units, carry


def _interleave(units, stages):
    per = -(-len(units) // len(stages))
    for i, stage in enumerate(stages):
        for unit in units[i * per:(i + 1) * per]:
            unit()
        stage()


def _valid_rows(first):
    row = lax.broadcasted_iota(jnp.int32, (CHUNK, 1), 0)
    return jnp.where(jnp.logical_or(jnp.logical_not(first), row >= PAD), 1.0, 0.0).astype(F32)


def _cum_masks(width):
    li = lax.broadcasted_iota(jnp.int32, (CHUNK, width), 0)
    si = lax.broadcasted_iota(jnp.int32, (CHUNK, width), 1) & (CHUNK - 1)
    return li, si


def _store_block_diag(w_ref, p):
    pb = p.astype(BF16)
    for j in range(p.shape[1] // CHUNK):
        js = slice(j * CHUNK, (j + 1) * CHUNK)
        w_ref[js, js] = pb[:, js]


def _ssd_kernel(zs_ref, xa_ref, xb_ref, sm_ref, cw_ref, cb_ref, dtb_ref, alog_ref, dexp_ref, nw_ref,
                tri_ref, e_ref, shift_ref, y_ref, halo_scr, xbc_scr, s_scr, wbd_scr):
    nb = zs_ref.shape[0]
    first = pl.program_id(0) == 0

    @pl.when(first)
    def _():
        halo_scr[...] = jnp.zeros(halo_scr.shape, BF16)
        s_scr[...] = jnp.zeros(s_scr.shape, F32)
        wbd_scr[...] = jnp.zeros(wbd_scr.shape, BF16)

    units, carries = [], []
    for b in range(nb):
        u, c = _conv_units((xa_ref.at[b], xb_ref.at[b]), halo_scr.at[b], shift_ref, cw_ref, cb_ref,
                           xbc_scr.at[b])
        units += u
        carries.append(c)

    lane = lax.broadcasted_iota(jnp.int32, (1, SMALL_W), 1)
    is_dt = lane < SSD_HEADS
    valid = _valid_rows(first)
    li, si = _cum_masks(SSD_D_INNER)
    st = [dict() for _ in range(nb)]

    def stage_dt():
        for b in range(nb):
            dt = jnp.where(is_dt, jax.nn.softplus(sm_ref[b] + dtb_ref[...]) * valid, 0.0)
            st[b]["dt"] = dt
            st[b]["a"] = dt * jnp.where(is_dt, -jnp.exp(alog_ref[...]), 0.0)

    def stage_cumsum():
        for b in range(nb):
            st[b]["acum"] = _dot_sel_right(tri_ref[...], st[b]["a"])

    def stage_expand():
        ex = _dot_sel_left(jnp.concatenate([t for b in range(nb) for t in (st[b]["dt"], st[b]["acum"])], axis=0),
                           e_ref[...])
        for b in range(nb):
            st[b]["ex"] = ex[2 * b * CHUNK:2 * (b + 1) * CHUNK]

    def stage_decay():
        for b in range(nb):
            ac_e = st[b]["ex"][CHUNK:2 * CHUNK]
            arow = jnp.sum(jnp.where(li == si, ac_e, 0.0), axis=0, keepdims=True)
            alast = ac_e[CHUNK - 1:CHUNK, :]
            st[b].update(dt_e=st[b]["ex"][0:CHUNK],
                         lmat=jnp.exp(jnp.where(li >= si, ac_e - arow, -jnp.inf)),
                         to_end=jnp.exp(alast - ac_e), from_start=jnp.exp(ac_e),
                         chunk_decay=jnp.exp(alast))

    _interleave(units, [stage_dt, stage_cumsum, stage_expand, stage_decay])
    for c in carries:
        c()

    c_off = SSD_D_INNER + SSD_GROUPS * SSD_STATE
    for g in range(SSD_GROUPS):
        cs = slice(g * SSD_GROUP_W, (g + 1) * SSD_GROUP_W)
        for b in range(nb):
            b_g = xbc_scr[b, :, SSD_D_INNER + g * SSD_STATE:SSD_D_INNER + (g + 1) * SSD_STATE]
            c_g = xbc_scr[b, :, c_off + g * SSD_STATE:c_off + (g + 1) * SSD_STATE]
            x_g = xbc_scr[b, :, cs]
            c_b = c_g.astype(BF16)
            b_b = b_g.astype(BF16)
            cb_t = _dot_nt(c_b, jnp.concatenate([b_b] * 8, axis=0))
            scores = cb_t * st[b]["lmat"][:, cs]
            xdt = x_g * st[b]["dt_e"][:, cs]
            y_parts = []
            for q in range(2):
                qs = slice(q * 4 * CHUNK, (q + 1) * 4 * CHUNK)
                w_bd = wbd_scr.at[(b * SSD_GROUPS + g) * 2 + q]
                _store_block_diag(w_bd, xdt[:, qs])
                y_parts.append(_dot(scores[:, qs].astype(BF16), w_bd[...]))
            y_diag = jnp.concatenate(y_parts, axis=1)
            s_old = s_scr[b, :, cs]
            y_off = _dot(c_b, s_old.astype(BF16)) * st[b]["from_start"][:, cs]
            xw = (xdt * st[b]["to_end"][:, cs]).astype(BF16)
            s_scr[b, :, cs] = s_old * st[b]["chunk_decay"][:, cs] + _dot(b_g.T.astype(BF16), xw)
            y = y_diag + y_off + dexp_ref[:, cs] * x_g
            y = y * _silu(zs_ref[b, :, cs].astype(F32))
            ms = jnp.mean(y * y, axis=-1, keepdims=True)
            y_ref[b, :, cs] = (y * lax.rsqrt(ms + NORM_EPS) * nw_ref[:, cs]).astype(BF16)


def _ssd(main, small, cw, cb, dtb, alog, dexp, nw, tri, e_ssd, shift):
    batch, t_pad, _ = main.shape
    nc = t_pad // CHUNK
    x_off = SSD_D_INNER + DN_VALUE_DIM + DN_CONV_DIM
    const = lambda c: (0, 0)
    return pl.pallas_call(
        _ssd_kernel,
        out_shape=jax.ShapeDtypeStruct((batch, (nc - 1) * CHUNK, SSD_D_INNER), BF16),
        grid=(nc,),
        in_specs=[pl.BlockSpec((batch, CHUNK, SSD_D_INNER), lambda c: (0, c, 0)),
                  pl.BlockSpec((batch, CHUNK, 2048), lambda c: (0, c, x_off // 2048)),
                  pl.BlockSpec((batch, CHUNK, 1024), lambda c: (0, c, (x_off + 2048) // 1024)),
                  pl.BlockSpec((batch, CHUNK, SMALL_W), lambda c: (0, c, 0)),
                  pl.BlockSpec((SSD_CONV, SSD_CONV_DIM), const),
                  pl.BlockSpec((1, SSD_CONV_DIM), const),
                  pl.BlockSpec((1, SMALL_W), const),
                  pl.BlockSpec((1, SMALL_W), const),
                  pl.BlockSpec((1, SSD_D_INNER), const),
                  pl.BlockSpec((1, SSD_D_INNER), const),
                  pl.BlockSpec((CHUNK, 3 * CHUNK), const),
                  pl.BlockSpec((3 * SMALL_W, SSD_D_INNER), const),
                  pl.BlockSpec(shift.shape, const)],
        out_specs=pl.BlockSpec((batch, CHUNK, SSD_D_INNER), lambda c: (0, jnp.maximum(c - 1, 0), 0)),
        scratch_shapes=[pltpu.VMEM((batch, CONV_HALO, SSD_CONV_DIM), BF16),
                        pltpu.VMEM((batch, CHUNK, SSD_CONV_DIM), F32),
                        pltpu.VMEM((batch, SSD_STATE, SSD_D_INNER), F32),
                        pltpu.VMEM((batch * SSD_HEADS // 4, 4 * CHUNK, 4 * CHUNK), BF16)],
        compiler_params=pltpu.CompilerParams(
            dimension_semantics=("arbitrary",), vmem_limit_bytes=VMEM_LIMIT),
        name="ssd_mixer",
    )(main, main, main, small, cw, cb, dtb, alog, dexp, nw, tri, e_ssd, shift)


def _l2norm(v):
    return v * lax.rsqrt(jnp.sum(v * v, axis=-1, keepdims=True) + NORM_EPS)


def _dn_kernel(zd_ref, qkv_ref, sm_ref, cw_ref, dtb_ref, alog_ref, nw_ref, tri_ref, e_ref, shift_ref,
               y_ref, halo_scr, qkv_scr, s_scr, wbd_scr):
    nb = zd_ref.shape[0]
    first = pl.program_id(0) == 0

    @pl.when(first)
    def _():
        halo_scr[...] = jnp.zeros(halo_scr.shape, BF16)
        s_scr[...] = jnp.zeros(s_scr.shape, F32)
        wbd_scr[...] = jnp.zeros(wbd_scr.shape, BF16)

    units, carries = [], []
    for b in range(nb):
        u, c = _conv_units((qkv_ref.at[b],), halo_scr.at[b], shift_ref, cw_ref, None, qkv_scr.at[b])
        units += u
        carries.append(c)

    lane = lax.broadcasted_iota(jnp.int32, (1, SMALL_W), 1)
    is_a = jnp.logical_and(lane >= COL_A, lane < COL_A + DN_V_HEADS)
    is_b = jnp.logical_and(lane >= COL_B, lane < COL_B + DN_V_HEADS)
    valid = _valid_rows(first)
    w64 = DN_V_HEADS * CHUNK
    li, si = _cum_masks(w64)
    gates = [dict() for _ in range(nb)]

    def stage_gate():
        for b in range(nb):
            sm = sm_ref[b]
            gates[b]["beta"] = jnp.where(is_b, jax.nn.sigmoid(sm) * valid, 0.0)
            neg_a = jnp.where(is_a, -jnp.exp(alog_ref[...]), 0.0)
            gates[b]["g"] = neg_a * jax.nn.softplus(sm + dtb_ref[...]) * valid

    def stage_cumsum():
        for b in range(nb):
            gates[b]["gcum"] = _dot_sel_right(tri_ref[...], gates[b]["g"])

    def stage_expand():
        ex = _dot_sel_left(jnp.concatenate([t for b in range(nb) for t in (gates[b]["beta"], gates[b]["gcum"])],
                                           axis=0), e_ref[...])
        for b in range(nb):
            gates[b]["ex"] = ex[2 * b * CHUNK:2 * (b + 1) * CHUNK]

    def stage_decay():
        for b in range(nb):
            ex = gates[b]["ex"]
            gc64 = ex[CHUNK:, 0:w64]
            gc128 = ex[CHUNK:, w64:]
            grow = jnp.sum(jnp.where(li == si, gc64, 0.0), axis=0, keepdims=True)
            glast = gc128[CHUNK - 1:CHUNK, :]
            gates[b].update(beta64=ex[0:CHUNK, 0:w64], beta128=ex[0:CHUNK, w64:],
                            dec=jnp.exp(jnp.where(li >= si, gc64 - grow, -jnp.inf)),
                            from_start=jnp.exp(gc128), to_end=jnp.exp(glast - gc128),
                            chunk_decay=jnp.exp(glast))

    _interleave(units, [stage_gate, stage_cumsum, stage_expand, stage_decay])
    for c in carries:
        c()

    li2, si2 = _cum_masks(2 * CHUNK)
    strict = li2 > si2
    eye2 = jnp.where(li2 == si2, 1.0, 0.0).astype(F32)

    q_heads, k_heads, n_mats, qkds = [], [], [], []
    for b in range(nb):
        for i in range(DN_QK_HEADS):
            q_i = _l2norm(qkv_scr[b, :, i * DN_HEAD:(i + 1) * DN_HEAD]) * (DN_HEAD ** -0.5)
            k_i = _l2norm(qkv_scr[b, :, DN_KEY_DIM + i * DN_HEAD:DN_KEY_DIM + (i + 1) * DN_HEAD])
            k_b = k_i.astype(BF16)
            prod = _dot_nt(jnp.concatenate([q_i.astype(BF16), k_b], axis=0),
                           jnp.concatenate([k_b, k_b], axis=0))
            ps = slice(i * 2 * CHUNK, (i + 1) * 2 * CHUNK)
            d_pair = gates[b]["dec"][:, ps]
            qkds.append(prod[0:CHUNK] * d_pair)
            n_mats.append(jnp.where(strict, -(gates[b]["beta64"][:, ps] * prod[CHUNK:] * d_pair), 0.0))
            q_heads.append(q_i)
            k_heads.append(k_i)

    pairs = range(nb * DN_QK_HEADS)
    t_mats = [eye2 + n for n in n_mats]

    def times_block_diag(lhs, p_list):
        for q in pairs:
            _store_block_diag(wbd_scr.at[q], p_list[q])
        return [_dot(lhs[q].astype(BF16), wbd_scr[q]) for q in pairs]

    p_mats = times_block_diag(n_mats, n_mats)
    for _ in range(4):
        boths = times_block_diag([jnp.concatenate([p_mats[q], t_mats[q]], axis=0) for q in pairs], p_mats)
        p_mats = [b[0:CHUNK] for b in boths]
        t_mats = [t + b[CHUNK:] for t, b in zip(t_mats, boths)]
    t_mats = [t + d for t, d in zip(t_mats, times_block_diag(t_mats, p_mats))]

    heads = [(b, h) for b in range(nb) for h in range(DN_V_HEADS)]
    hsl = [slice(h * DN_HEAD, (h + 1) * DN_HEAD) for h in range(DN_V_HEADS)]
    uws = []
    for n, (b, h) in enumerate(heads):
        k_i = k_heads[n // 2]
        v_h = qkv_scr[b, :, 2 * DN_KEY_DIM + h * DN_HEAD:2 * DN_KEY_DIM + (h + 1) * DN_HEAD]
        b_h = gates[b]["beta128"][:, hsl[h]]
        rhs = jnp.concatenate([v_h * b_h, k_i * (b_h * gates[b]["from_start"][:, hsl[h]])], axis=1)
        j = h % 2
        t_h = t_mats[n // 2][:, j * CHUNK:(j + 1) * CHUNK].astype(BF16)
        uws.append(_dot(t_h, rhs.astype(BF16)))

    s_olds = [s_scr[n] for n in range(len(heads))]
    ws_qs = [_dot(jnp.concatenate([uws[n][:, DN_HEAD:], q_heads[n // 2] * gates[b]["from_start"][:, hsl[h]]],
                                  axis=0).astype(BF16), s_olds[n].astype(BF16))
             for n, (b, h) in enumerate(heads)]
    v_news = [(uws[n][:, 0:DN_HEAD] - ws_qs[n][0:CHUNK]).astype(BF16) for n in range(len(heads))]
    for n, (b, h) in enumerate(heads):
        k_dec = k_heads[n // 2] * gates[b]["to_end"][:, hsl[h]]
        s_scr[n] = s_olds[n] * gates[b]["chunk_decay"][:, hsl[h]] + _dot(k_dec.T.astype(BF16), v_news[n])
        j = h % 2
        o_h = ws_qs[n][CHUNK:] + _dot(qkds[n // 2][:, j * CHUNK:(j + 1) * CHUNK].astype(BF16), v_news[n])
        ms = jnp.mean(o_h * o_h, axis=-1, keepdims=True)
        y = o_h * lax.rsqrt(ms + NORM_EPS) * nw_ref[...]
        y_ref[b, :, hsl[h]] = (y * _silu(zd_ref[b, :, hsl[h]].astype(F32))).astype(BF16)


def _dn(main, small, cw, dtb, alog, nw, tri, e_dn, shift):
    batch, t_pad, _ = main.shape
    nc = t_pad // CHUNK
    const = lambda c: (0, 0)
    return pl.pallas_call(
        _dn_kernel,
        out_shape=jax.ShapeDtypeStruct((batch, (nc - 1) * CHUNK, DN_VALUE_DIM), BF16),
        grid=(nc,),
        in_specs=[pl.BlockSpec((batch, CHUNK, DN_VALUE_DIM), lambda c: (0, c, SSD_D_INNER // DN_VALUE_DIM)),
                  pl.BlockSpec((batch, CHUNK, DN_CONV_DIM),
                               lambda c: (0, c, (SSD_D_INNER + DN_VALUE_DIM) // DN_CONV_DIM)),
                  pl.BlockSpec((batch, CHUNK, SMALL_W), lambda c: (0, c, 0)),
                  pl.BlockSpec((DN_CONV, DN_CONV_DIM), const),
                  pl.BlockSpec((1, SMALL_W), const),
                  pl.BlockSpec((1, SMALL_W), const),
                  pl.BlockSpec((1, DN_HEAD), const),
                  pl.BlockSpec((CHUNK, 3 * CHUNK), const),
                  pl.BlockSpec(e_dn.shape, const),
                  pl.BlockSpec(shift.shape, const)],
        out_specs=pl.BlockSpec((batch, CHUNK, DN_VALUE_DIM), lambda c: (0, jnp.maximum(c - 1, 0), 0)),
        scratch_shapes=[pltpu.VMEM((batch, CONV_HALO, DN_CONV_DIM), BF16),
                        pltpu.VMEM((batch, CHUNK, DN_CONV_DIM), F32),
                        pltpu.VMEM((batch * DN_V_HEADS, DN_HEAD, DN_HEAD), F32),
                        pltpu.VMEM((batch * DN_QK_HEADS, 2 * CHUNK, 2 * CHUNK), BF16)],
        compiler_params=pltpu.CompilerParams(
            dimension_semantics=("arbitrary",), vmem_limit_bytes=VMEM_LIMIT),
        name="dn_mixer",
    )(main, main, small, cw, dtb, alog, nw, tri, e_dn, shift)


def _rms(v):
    return v * lax.rsqrt(jnp.mean(v * v, axis=-1, keepdims=True) + NORM_EPS)


def _dense_kernel(x_ref, ys_ref, yd_ref, nw_ref, wg_ref, wb0_ref, wb1_ref, wo_ref, fnw_ref, wgu_ref,
                  wd_ref, fw_ref, o_ref):
    x = x_ref[...]
    u = (_rms(x) * nw_ref[...]).astype(BF16)
    gates = jax.nn.sigmoid(_dot(u, wg_ref[...]))
    merged = (gates[:, 0:D_MODEL] * _dot(ys_ref[...], wb0_ref[...])
              + gates[:, D_MODEL:] * _dot(yd_ref[...], wb1_ref[...]))
    h1 = x + _dot(merged.astype(BF16), wo_ref[...])
    u2 = (_rms(h1) * fnw_ref[...]).astype(BF16)
    gu = _dot(u2, wgu_ref[...])
    act = (_silu(gu[:, 0:D_FF]) * gu[:, D_FF:]).astype(BF16)
    h2 = h1 + _dot(act, wd_ref[...])
    o_ref[...] = _rms(h2) * fw_ref[...]


def _dense(x2d, ys, yd, nw, wg, wb0, wb1, wo, fnw, wgu, wd, fw, tm):
    m = x2d.shape[0]
    rows = lambda i: (i, 0)

    def resident(shape):
        return pl.BlockSpec(shape, lambda i: (0, 0), pipeline_mode=pl.Buffered(1))

    return pl.pallas_call(
        _dense_kernel,
        out_shape=jax.ShapeDtypeStruct((m, D_MODEL), F32),
        grid=(m // tm,),
        in_specs=[pl.BlockSpec((tm, D_MODEL), rows),
                  pl.BlockSpec((tm, SSD_D_INNER), rows),
                  pl.BlockSpec((tm, DN_VALUE_DIM), rows),
                  resident((1, D_MODEL)),
                  resident((D_MODEL, 2 * D_MODEL)),
                  resident((SSD_D_INNER, D_MODEL)),
                  resident((DN_VALUE_DIM, D_MODEL)),
                  resident((D_MODEL, D_MODEL)),
                  resident((1, D_MODEL)),
                  resident((D_MODEL, 2 * D_FF)),
                  resident((D_FF, D_MODEL)),
                  resident((1, D_MODEL))],
        out_specs=pl.BlockSpec((tm, D_MODEL), rows),
        compiler_params=pltpu.CompilerParams(
            dimension_semantics=("arbitrary",), vmem_limit_bytes=VMEM_LIMIT),
        name="merge_ffn",
    )(x2d, ys, yd, nw, wg, wb0, wb1, wo, fnw, wgu, wd, fw)


_O_XBC = SSD_D_INNER
_O_DT = _O_XBC + SSD_CONV_DIM
_O_QKV = _O_DT + SSD_HEADS
_O_A = _O_QKV + DN_CONV_DIM
_O_ZD = _O_A + 2 * DN_V_HEADS
_O_GATE = _O_ZD + DN_VALUE_DIM
D_IN_PROJ = _O_GATE + 2 * D_MODEL
WPREP_ROWS = 128


def _wprep_kernel(w_ref, main_ref, small_ref, gate_ref):
    def cols(lo, hi):
        return w_ref[:, lo:hi].astype(BF16)

    main_ref[:, 0:SSD_D_INNER] = cols(0, _O_XBC)
    main_ref[:, SSD_D_INNER:SSD_D_INNER + DN_VALUE_DIM] = cols(_O_ZD, _O_GATE)
    q0 = SSD_D_INNER + DN_VALUE_DIM
    main_ref[:, q0:q0 + DN_CONV_DIM] = cols(_O_QKV, _O_A)
    main_ref[:, q0 + DN_CONV_DIM:MAIN_W] = cols(_O_XBC, _O_DT)
    n_small = SSD_HEADS + 2 * DN_V_HEADS
    small_ref[...] = jnp.concatenate(
        [cols(_O_DT, _O_QKV), cols(_O_A, _O_ZD), jnp.zeros((w_ref.shape[0], SMALL_W - n_small), BF16)], axis=1)
    gate_ref[...] = cols(_O_GATE, D_IN_PROJ)


def _wprep(w):
    rows = lambda i: (i, 0)
    return pl.pallas_call(
        _wprep_kernel,
        out_shape=(jax.ShapeDtypeStruct((D_MODEL, MAIN_W), BF16),
                   jax.ShapeDtypeStruct((D_MODEL, SMALL_W), BF16),
                   jax.ShapeDtypeStruct((D_MODEL, 2 * D_MODEL), BF16)),
        grid=(D_MODEL // WPREP_ROWS,),
        in_specs=[pl.BlockSpec((WPREP_ROWS, D_IN_PROJ), rows)],
        out_specs=(pl.BlockSpec((WPREP_ROWS, MAIN_W), rows),
                   pl.BlockSpec((WPREP_ROWS, SMALL_W), rows),
                   pl.BlockSpec((WPREP_ROWS, 2 * D_MODEL), rows)),
        compiler_params=pltpu.CompilerParams(
            dimension_semantics=("arbitrary",), vmem_limit_bytes=VMEM_LIMIT),
        name="inproj_weight_relayout",
    )(w)


def _pad_lanes(vec, offset):
    out = jnp.zeros((1, SMALL_W), F32)
    return lax.dynamic_update_slice(out, vec.astype(F32)[None, :], (0, offset))


def _expand_matrix(src_rows, group, heads):
    e = np.zeros((SMALL_W, heads * group), np.float32)
    for h in range(heads):
        e[src_rows + h, h * group:(h + 1) * group] = 1.0
    return e


def _shift_matrix():
    s = np.zeros(((SSD_CONV - 1) * CHUNK, CONV_HALO + CHUNK), np.float32)
    for k in range(SSD_CONV - 1):
        for t in range(CHUNK):
            s[k * CHUNK + t, CONV_HALO + t - (SSD_CONV - 1) + k] = 1.0
    return s


def _row_tile(m, target):
    t = min(target, m) // BF16_SUBLANES * BF16_SUBLANES
    while m % t:
        t -= BF16_SUBLANES
    return t


def kernel(x, meta_tokens, mix_norm_w, w_in, ssd_conv_w, ssd_conv_b, ssd_dt_bias, ssd_a_log, ssd_d,
           ssd_norm_w, dn_conv_w, dn_dt_bias, dn_a_log, dn_norm_w, w_branch, w_out, ffn_norm_w,
           w_gate_up, w_down, final_norm_w):
    batch, seq, _ = x.shape
    t_pad = PAD + N_META + seq
    assert t_pad % CHUNK == 0 and mix_norm_w.shape[0] == 1

    head = jnp.concatenate([jnp.zeros((PAD, D_MODEL), x.dtype), meta_tokens.astype(x.dtype)], axis=0)
    h_pad = jnp.concatenate([jnp.broadcast_to(head[None], (batch, CHUNK, D_MODEL)), x], axis=1)
    h_pad = h_pad.reshape(batch * t_pad, D_MODEL)

    w_main, w_small, w_gates = _wprep(w_in[0])

    m_pad = batch * t_pad
    main, small = _inproj(h_pad, mix_norm_w.astype(F32), w_main, w_small,
                          _row_tile(m_pad, INPROJ_ROWS), INPROJ_COLS)

    tri = jnp.asarray(np.tile(np.tril(np.ones((CHUNK, CHUNK), np.float32)), (1, 3)), BF16)
    shift = jnp.asarray(_shift_matrix(), BF16)
    e_ssd = jnp.asarray(np.tile(_expand_matrix(COL_DT, SSD_HEAD_DIM, SSD_HEADS), (3, 1)), BF16)
    e_dn = jnp.asarray(np.tile(
        np.concatenate([_expand_matrix(COL_B, CHUNK, DN_V_HEADS) + _expand_matrix(COL_A, CHUNK, DN_V_HEADS),
                        _expand_matrix(COL_B, DN_HEAD, DN_V_HEADS) + _expand_matrix(COL_A, DN_HEAD, DN_V_HEADS)],
                       axis=1), (3, 1)), BF16)

    main = main.reshape(batch, t_pad, MAIN_W)
    small = small.reshape(batch, t_pad, SMALL_W)
    y_s = _ssd(main, small, ssd_conv_w[0].astype(F32), ssd_conv_b.astype(F32),
               _pad_lanes(ssd_dt_bias[0], COL_DT), _pad_lanes(ssd_a_log[0], COL_DT),
               jnp.repeat(ssd_d[0].astype(F32), SSD_HEAD_DIM)[None, :], ssd_norm_w.astype(F32),
               tri, e_ssd, shift)
    y_d = _dn(main, small, dn_conv_w[0].astype(F32), _pad_lanes(dn_dt_bias[0], COL_A),
              _pad_lanes(dn_a_log[0], COL_A), dn_norm_w.astype(F32), tri, e_dn, shift)

    x2d = x.reshape(batch * seq, D_MODEL)
    y_s = y_s.reshape(batch * seq, SSD_D_INNER)
    y_d = y_d.reshape(batch * seq, DN_VALUE_DIM)
    out = _dense(x2d, y_s, y_d, mix_norm_w.astype(F32), w_gates, w_branch[0, 0].astype(BF16),
                 w_branch[0, 1].astype(BF16), w_out[0].astype(BF16), ffn_norm_w.astype(F32),
                 w_gate_up[0].astype(BF16), w_down[0].astype(BF16), final_norm_w.astype(F32)[None, :],
                 _row_tile(batch * seq, DENSE_ROWS))
    return out.reshape(batch, seq, D_MODEL)
```

```python
import numpy as np
import jax
import jax.numpy as jnp
from jax import lax
from jax.experimental import pallas as pl
from jax.experimental.pallas import tpu as pltpu

F32 = jnp.float32
BF16 = jnp.bfloat16

D_MODEL = 1024
CHUNK = 64
N_META = 16
PAD = CHUNK - N_META
NORM_EPS = 1e-6

SSD_D_INNER = 2048
SSD_HEAD_DIM = 64
SSD_HEADS = 32
SSD_GROUPS = 4
SSD_STATE = 128
SSD_CONV = 4
SSD_CONV_DIM = 3072
SSD_GROUP_W = SSD_D_INNER // SSD_GROUPS

DN_QK_HEADS = 8
DN_V_HEADS = 16
DN_HEAD = 128
DN_KEY_DIM = 1024
DN_VALUE_DIM = 2048
DN_CONV = 4
DN_CONV_DIM = 4096

D_FF = 2816
SMALL_W = 128
COL_DT, COL_A, COL_B = 0, 32, 48
MAIN_W = SSD_D_INNER + DN_VALUE_DIM + DN_CONV_DIM + SSD_CONV_DIM
CONV_HALO = 16
CONV_COLS = 256
VMEM_LIMIT = 56 * 1024 * 1024
BF16_SUBLANES = 16
INPROJ_ROWS, INPROJ_COLS = 1376, 1024
DENSE_ROWS = 512


def _silu(v):
    return v * jax.nn.sigmoid(v)


def _dot(a, b):
    return jnp.dot(a, b, preferred_element_type=F32)


def _dot_nt(a, b):
    return lax.dot_general(a, b, (((1,), (1,)), ((), ())), preferred_element_type=F32)


def _split3(v):
    p1 = v.astype(BF16)
    r1 = v - p1.astype(F32)
    p2 = r1.astype(BF16)
    p3 = (r1 - p2.astype(F32)).astype(BF16)
    return p1, p2, p3


def _dot_sel_right(sel3, v):
    return _dot(sel3, jnp.concatenate(_split3(v), axis=0))


def _dot_sel_left(v, sel3):
    return _dot(jnp.concatenate(_split3(v), axis=1), sel3)


def _inproj_kernel(h_ref, nw_ref, w_ref, ws_ref, o_ref, os_ref, u_scr):
    @pl.when(pl.program_id(1) == 0)
    def _():
        x = h_ref[...]
        ms = jnp.mean(x * x, axis=-1, keepdims=True)
        u = (x * lax.rsqrt(ms + NORM_EPS) * nw_ref[...]).astype(BF16)
        u_scr[...] = u
        os_ref[...] = _dot(u, ws_ref[...])

    o_ref[...] = _dot(u_scr[...], w_ref[...]).astype(BF16)


def _inproj(h_pad, norm_w, w_all, w_small, tm, tn):
    m = h_pad.shape[0]
    assert (2 * D_MODEL) % tn == 0
    return pl.pallas_call(
        _inproj_kernel,
        out_shape=(jax.ShapeDtypeStruct((m, MAIN_W), BF16),
                   jax.ShapeDtypeStruct((m, SMALL_W), F32)),
        grid=(m // tm, MAIN_W // tn),
        in_specs=[pl.BlockSpec((tm, D_MODEL), lambda i, j: (i, 0)),
                  pl.BlockSpec((1, D_MODEL), lambda i, j: (0, 0)),
                  pl.BlockSpec((D_MODEL, tn), lambda i, j: (0, j + 2 * D_MODEL // tn)),
                  pl.BlockSpec((D_MODEL, SMALL_W), lambda i, j: (0, 0))],
        out_specs=(pl.BlockSpec((tm, tn), lambda i, j: (i, j)),
                   pl.BlockSpec((tm, SMALL_W), lambda i, j: (i, 0))),
        scratch_shapes=[pltpu.VMEM((tm, D_MODEL), BF16)],
        compiler_params=pltpu.CompilerParams(
            dimension_semantics=("arbitrary", "arbitrary"), vmem_limit_bytes=VMEM_LIMIT),
        name="inproj",
    )(h_pad, norm_w, w_all, w_small)


def _conv_units(raw_refs, halo_scr, shift_ref, cw_ref, bias_ref, out_scr):
    units = []
    col = 0
    for r in raw_refs:
        for c0 in range(0, r.shape[1], CONV_COLS):
            def unit(r=r, c0=c0, cs=slice(col + c0, col + c0 + CONV_COLS)):
                xe = jnp.concatenate([halo_scr[:, cs], r[:, c0:c0 + CONV_COLS]], axis=0)
                taps = _dot(shift_ref[...], xe)
                acc = cw_ref[SSD_CONV - 1:SSD_CONV, cs] * r[:, c0:c0 + CONV_COLS].astype(F32)
                for k in range(SSD_CONV - 1):
                    acc = acc + cw_ref[k:k + 1, cs] * taps[k * CHUNK:(k + 1) * CHUNK]
                if bias_ref is not None:
                    acc = acc + bias_ref[:, cs]
                out_scr[:, cs] = _silu(acc)
            units.append(unit)
        col += r.shape[1]

    def carry():
        col = 0
        for r in raw_refs:
            halo_scr[:, col:col + r.shape[1]] = r[CHUNK - CONV_HALO:CHUNK, :]
            col += r.shape[1]

    return units, carry


def _interleave(units, stages):
    per = -(-len(units) // len(stages))
    for i, stage in enumerate(stages):
        for unit in units[i * per:(i + 1) * per]:
            unit()
        stage()


def _valid_rows(first):
    row = lax.broadcasted_iota(jnp.int32, (CHUNK, 1), 0)
    return jnp.where(jnp.logical_or(jnp.logical_not(first), row >= PAD), 1.0, 0.0).astype(F32)


def _cum_masks(width):
    li = lax.broadcasted_iota(jnp.int32, (CHUNK, width), 0)
    si = lax.broadcasted_iota(jnp.int32, (CHUNK, width), 1) & (CHUNK - 1)
    return li, si


def _store_block_diag(w_ref, p):
    pb = p.astype(BF16)
    for j in range(p.shape[1] // CHUNK):
        js = slice(j * CHUNK, (j + 1) * CHUNK)
        w_ref[js, js] = pb[:, js]


def _ssd_kernel(zs_ref, xa_ref, xb_ref, sm_ref, cw_ref, cb_ref, dtb_ref, alog_ref, dexp_ref, nw_ref,
                tri_ref, e_ref, shift_ref, y_ref, halo_scr, xbc_scr, s_scr, wbd_scr):
    nb = zs_ref.shape[0]
    first = pl.program_id(0) == 0

    @pl.when(first)
    def _():
        halo_scr[...] = jnp.zeros(halo_scr.shape, BF16)
        s_scr[...] = jnp.zeros(s_scr.shape, F32)
        wbd_scr[...] = jnp.zeros(wbd_scr.shape, BF16)

    units, carries = [], []
    for b in range(nb):
        u, c = _conv_units((xa_ref.at[b], xb_ref.at[b]), halo_scr.at[b], shift_ref, cw_ref, cb_ref,
                           xbc_scr.at[b])
        units += u
        carries.append(c)

    lane = lax.broadcasted_iota(jnp.int32, (1, SMALL_W), 1)
    is_dt = lane < SSD_HEADS
    valid = _valid_rows(first)
    li, si = _cum_masks(SSD_D_INNER)
    st = [dict() for _ in range(nb)]

    def stage_dt():
        for b in range(nb):
            dt = jnp.where(is_dt, jax.nn.softplus(sm_ref[b] + dtb_ref[...]) * valid, 0.0)
            st[b]["dt"] = dt
            st[b]["a"] = dt * jnp.where(is_dt, -jnp.exp(alog_ref[...]), 0.0)

    def stage_cumsum():
        for b in range(nb):
            st[b]["acum"] = _dot_sel_right(tri_ref[...], st[b]["a"])

    def stage_expand():
        ex = _dot_sel_left(jnp.concatenate([t for b in range(nb) for t in (st[b]["dt"], st[b]["acum"])], axis=0),
                           e_ref[...])
        for b in range(nb):
            st[b]["ex"] = ex[2 * b * CHUNK:2 * (b + 1) * CHUNK]

    def stage_decay():
        for b in range(nb):
            ac_e = st[b]["ex"][CHUNK:2 * CHUNK]
            arow = jnp.sum(jnp.where(li == si, ac_e, 0.0), axis=0, keepdims=True)
            alast = ac_e[CHUNK - 1:CHUNK, :]
            st[b].update(dt_e=st[b]["ex"][0:CHUNK],
                         lmat=jnp.exp(jnp.where(li >= si, ac_e - arow, -jnp.inf)),
                         to_end=jnp.exp(alast - ac_e), from_start=jnp.exp(ac_e),
                         chunk_decay=jnp.exp(alast))

    _interleave(units, [stage_dt, stage_cumsum, stage_expand, stage_decay])
    for c in carries:
        c()

    c_off = SSD_D_INNER + SSD_GROUPS * SSD_STATE
    for g in range(SSD_GROUPS):
        cs = slice(g * SSD_GROUP_W, (g + 1) * SSD_GROUP_W)
        for b in range(nb):
            b_g = xbc_scr[b, :, SSD_D_INNER + g * SSD_STATE:SSD_D_INNER + (g + 1) * SSD_STATE]
            c_g = xbc_scr[b, :, c_off + g * SSD_STATE:c_off + (g + 1) * SSD_STATE]
            x_g = xbc_scr[b, :, cs]
            c_b = c_g.astype(BF16)
            b_b = b_g.astype(BF16)
            cb_t = _dot_nt(c_b, jnp.concatenate([b_b] * 8, axis=0))
            scores = cb_t * st[b]["lmat"][:, cs]
            xdt = x_g * st[b]["dt_e"][:, cs]
            y_parts = []
            for q in range(2):
                qs = slice(q * 4 * CHUNK, (q + 1) * 4 * CHUNK)
                w_bd = wbd_scr.at[(b * SSD_GROUPS + g) * 2 + q]
                _store_block_diag(w_bd, xdt[:, qs])
                y_parts.append(_dot(scores[:, qs].astype(BF16), w_bd[...]))
            y_diag = jnp.concatenate(y_parts, axis=1)
            s_old = s_scr[b, :, cs]
            y_off = _dot(c_b, s_old.astype(BF16)) * st[b]["from_start"][:, cs]
            xw = (xdt * st[b]["to_end"][:, cs]).astype(BF16)
            s_scr[b, :, cs] = s_old * st[b]["chunk_decay"][:, cs] + _dot(b_g.T.astype(BF16), xw)
            y = y_diag + y_off + dexp_ref[:, cs] * x_g
            y = y * _silu(zs_ref[b, :, cs].astype(F32))
            ms = jnp.mean(y * y, axis=-1, keepdims=True)
            y_ref[b, :, cs] = (y * lax.rsqrt(ms + NORM_EPS) * nw_ref[:, cs]).astype(BF16)


def _ssd(main, small, cw, cb, dtb, alog, dexp, nw, tri, e_ssd, shift):
    batch, t_pad, _ = main.shape
    nc = t_pad // CHUNK
    x_off = SSD_D_INNER + DN_VALUE_DIM + DN_CONV_DIM
    const = lambda c: (0, 0)
    return pl.pallas_call(
        _ssd_kernel,
        out_shape=jax.ShapeDtypeStruct((batch, (nc - 1) * CHUNK, SSD_D_INNER), BF16),
        grid=(nc,),
        in_specs=[pl.BlockSpec((batch, CHUNK, SSD_D_INNER), lambda c: (0, c, 0)),
                  pl.BlockSpec((batch, CHUNK, 2048), lambda c: (0, c, x_off // 2048)),
                  pl.BlockSpec((batch, CHUNK, 1024), lambda c: (0, c, (x_off + 2048) // 1024)),
                  pl.BlockSpec((batch, CHUNK, SMALL_W), lambda c: (0, c, 0)),
                  pl.BlockSpec((SSD_CONV, SSD_CONV_DIM), const),
                  pl.BlockSpec((1, SSD_CONV_DIM), const),
                  pl.BlockSpec((1, SMALL_W), const),
                  pl.BlockSpec((1, SMALL_W), const),
                  pl.BlockSpec((1, SSD_D_INNER), const),
                  pl.BlockSpec((1, SSD_D_INNER), const),
                  pl.BlockSpec((CHUNK, 3 * CHUNK), const),
                  pl.BlockSpec((3 * SMALL_W, SSD_D_INNER), const),
                  pl.BlockSpec(shift.shape, const)],
        out_specs=pl.BlockSpec((batch, CHUNK, SSD_D_INNER), lambda c: (0, jnp.maximum(c - 1, 0), 0)),
        scratch_shapes=[pltpu.VMEM((batch, CONV_HALO, SSD_CONV_DIM), BF16),
                        pltpu.VMEM((batch, CHUNK, SSD_CONV_DIM), F32),
                        pltpu.VMEM((batch, SSD_STATE, SSD_D_INNER), F32),
                        pltpu.VMEM((batch * SSD_HEADS // 4, 4 * CHUNK, 4 * CHUNK), BF16)],
        compiler_params=pltpu.CompilerParams(
            dimension_semantics=("arbitrary",), vmem_limit_bytes=VMEM_LIMIT),
        name="ssd_mixer",
    )(main, main, main, small, cw, cb, dtb, alog, dexp, nw, tri, e_ssd, shift)


def _l2norm(v):
    return v * lax.rsqrt(jnp.sum(v * v, axis=-1, keepdims=True) + NORM_EPS)


def _dn_kernel(zd_ref, qkv_ref, sm_ref, cw_ref, dtb_ref, alog_ref, nw_ref, tri_ref, e_ref, shift_ref,
               y_ref, halo_scr, qkv_scr, s_scr, wbd_scr):
    nb = zd_ref.shape[0]
    first = pl.program_id(0) == 0

    @pl.when(first)
    def _():
        halo_scr[...] = jnp.zeros(halo_scr.shape, BF16)
        s_scr[...] = jnp.zeros(s_scr.shape, F32)
        wbd_scr[...] = jnp.zeros(wbd_scr.shape, BF16)

    units, carries = [], []
    for b in range(nb):
        u, c = _conv_units((qkv_ref.at[b],), halo_scr.at[b], shift_ref, cw_ref, None, qkv_scr.at[b])
        units += u
        carries.append(c)

    lane = lax.broadcasted_iota(jnp.int32, (1, SMALL_W), 1)
    is_a = jnp.logical_and(lane >= COL_A, lane < COL_A + DN_V_HEADS)
    is_b = jnp.logical_and(lane >= COL_B, lane < COL_B + DN_V_HEADS)
    valid = _valid_rows(first)
    w64 = DN_V_HEADS * CHUNK
    li, si = _cum_masks(w64)
    gates = [dict() for _ in range(nb)]

    def stage_gate():
        for b in range(nb):
            sm = sm_ref[b]
            gates[b]["beta"] = jnp.where(is_b, jax.nn.sigmoid(sm) * valid, 0.0)
            neg_a = jnp.where(is_a, -jnp.exp(alog_ref[...]), 0.0)
            gates[b]["g"] = neg_a * jax.nn.softplus(sm + dtb_ref[...]) * valid

    def stage_cumsum():
        for b in range(nb):
            gates[b]["gcum"] = _dot_sel_right(tri_ref[...], gates[b]["g"])

    def stage_expand():
        ex = _dot_sel_left(jnp.concatenate([t for b in range(nb) for t in (gates[b]["beta"], gates[b]["gcum"])],
                                           axis=0), e_ref[...])
        for b in range(nb):
            gates[b]["ex"] = ex[2 * b * CHUNK:2 * (b + 1) * CHUNK]

    def stage_decay():
        for b in range(nb):
            ex = gates[b]["ex"]
            gc64 = ex[CHUNK:, 0:w64]
            gc128 = ex[CHUNK:, w64:]
            grow = jnp.sum(jnp.where(li == si, gc64, 0.0), axis=0, keepdims=True)
            glast = gc128[CHUNK - 1:CHUNK, :]
            gates[b].update(beta64=ex[0:CHUNK, 0:w64], beta128=ex[0:CHUNK, w64:],
                            dec=jnp.exp(jnp.where(li >= si, gc64 - grow, -jnp.inf)),
                            from_start=jnp.exp(gc128), to_end=jnp.exp(glast - gc128),
                            chunk_decay=jnp.exp(glast))

    _interleave(units, [stage_gate, stage_cumsum, stage_expand, stage_decay])
    for c in carries:
        c()

    li2, si2 = _cum_masks(2 * CHUNK)
    strict = li2 > si2
    eye2 = jnp.where(li2 == si2, 1.0, 0.0).astype(F32)

    q_heads, k_heads, n_mats, qkds = [], [], [], []
    for b in range(nb):
        for i in range(DN_QK_HEADS):
            q_i = _l2norm(qkv_scr[b, :, i * DN_HEAD:(i + 1) * DN_HEAD]) * (DN_HEAD ** -0.5)
            k_i = _l2norm(qkv_scr[b, :, DN_KEY_DIM + i * DN_HEAD:DN_KEY_DIM + (i + 1) * DN_HEAD])
            k_b = k_i.astype(BF16)
            prod = _dot_nt(jnp.concatenate([q_i.astype(BF16), k_b], axis=0),
                           jnp.concatenate([k_b, k_b], axis=0))
            ps = slice(i * 2 * CHUNK, (i + 1) * 2 * CHUNK)
            d_pair = gates[b]["dec"][:, ps]
            qkds.append(prod[0:CHUNK] * d_pair)
            n_mats.append(jnp.where(strict, -(gates[b]["beta64"][:, ps] * prod[CHUNK:] * d_pair), 0.0))
            q_heads.append(q_i)
            k_heads.append(k_i)

    pairs = range(nb * DN_QK_HEADS)
    t_mats = [eye2 + n for n in n_mats]

    def times_block_diag(lhs, p_list):
        for q in pairs:
            _store_block_diag(wbd_scr.at[q], p_list[q])
        return [_dot(lhs[q].astype(BF16), wbd_scr[q]) for q in pairs]

    p_mats = times_block_diag(n_mats, n_mats)
    for _ in range(4):
        boths = times_block_diag([jnp.concatenate([p_mats[q], t_mats[q]], axis=0) for q in pairs], p_mats)
        p_mats = [b[0:CHUNK] for b in boths]
        t_mats = [t + b[CHUNK:] for t, b in zip(t_mats, boths)]
    t_mats = [t + d for t, d in zip(t_mats, times_block_diag(t_mats, p_mats))]

    heads = [(b, h) for b in range(nb) for h in range(DN_V_HEADS)]
    hsl = [slice(h * DN_HEAD, (h + 1) * DN_HEAD) for h in range(DN_V_HEADS)]
    uws = []
    for n, (b, h) in enumerate(heads):
        k_i = k_heads[n // 2]
        v_h = qkv_scr[b, :, 2 * DN_KEY_DIM + h * DN_HEAD:2 * DN_KEY_DIM + (h + 1) * DN_HEAD]
        b_h = gates[b]["beta128"][:, hsl[h]]
        rhs = jnp.concatenate([v_h * b_h, k_i * (b_h * gates[b]["from_start"][:, hsl[h]])], axis=1)
        j = h % 2
        t_h = t_mats[n // 2][:, j * CHUNK:(j + 1) * CHUNK].astype(BF16)
        uws.append(_dot(t_h, rhs.astype(BF16)))

    s_olds = [s_scr[n] for n in range(len(heads))]
    ws_qs = [_dot(jnp.concatenate([uws[n][:, DN_HEAD:], q_heads[n // 2] * gates[b]["from_start"][:, hsl[h]]],
                                  axis=0).astype(BF16), s_olds[n].astype(BF16))
             for n, (b, h) in enumerate(heads)]
    v_news = [(uws[n][:, 0:DN_HEAD] - ws_qs[n][0:CHUNK]).astype(BF16) for n in range(len(heads))]
    for n, (b, h) in enumerate(heads):
        k_dec = k_heads[n // 2] * gates[b]["to_end"][:, hsl[h]]
        s_scr[n] = s_olds[n] * gates[b]["chunk_decay"][:, hsl[h]] + _dot(k_dec.T.astype(BF16), v_news[n])
        j = h % 2
        o_h = ws_qs[n][CHUNK:] + _dot(qkds[n // 2][:, j * CHUNK:(j + 1) * CHUNK].astype(BF16), v_news[n])
        ms = jnp.mean(o_h * o_h, axis=-1, keepdims=True)
        y = o_h * lax.rsqrt(ms + NORM_EPS) * nw_ref[...]
        y_ref[b, :, hsl[h]] = (y * _silu(zd_ref[b, :, hsl[h]].astype(F32))).astype(BF16)


def _dn(main, small, cw, dtb, alog, nw, tri, e_dn, shift):
    batch, t_pad, _ = main.shape
    nc = t_pad // CHUNK
    const = lambda c: (0, 0)
    return pl.pallas_call(
        _dn_kernel,
        out_shape=jax.ShapeDtypeStruct((batch, (nc - 1) * CHUNK, DN_VALUE_DIM), BF16),
        grid=(nc,),
        in_specs=[pl.BlockSpec((batch, CHUNK, DN_VALUE_DIM), lambda c: (0, c, SSD_D_INNER // DN_VALUE_DIM)),
                  pl.BlockSpec((batch, CHUNK, DN_CONV_DIM),
                               lambda c: (0, c, (SSD_D_INNER + DN_VALUE_DIM) // DN_CONV_DIM)),
                  pl.BlockSpec((batch, CHUNK, SMALL_W), lambda c: (0, c, 0)),
                  pl.BlockSpec((DN_CONV, DN_CONV_DIM), const),
                  pl.BlockSpec((1, SMALL_W), const),
                  pl.BlockSpec((1, SMALL_W), const),
                  pl.BlockSpec((1, DN_HEAD), const),
                  pl.BlockSpec((CHUNK, 3 * CHUNK), const),
                  pl.BlockSpec(e_dn.shape, const),
                  pl.BlockSpec(shift.shape, const)],
        out_specs=pl.BlockSpec((batch, CHUNK, DN_VALUE_DIM), lambda c: (0, jnp.maximum(c - 1, 0), 0)),
        scratch_shapes=[pltpu.VMEM((batch, CONV_HALO, DN_CONV_DIM), BF16),
                        pltpu.VMEM((batch, CHUNK, DN_CONV_DIM), F32),
                        pltpu.VMEM((batch * DN_V_HEADS, DN_HEAD, DN_HEAD), F32),
                        pltpu.VMEM((batch * DN_QK_HEADS, 2 * CHUNK, 2 * CHUNK), BF16)],
        compiler_params=pltpu.CompilerParams(
            dimension_semantics=("arbitrary",), vmem_limit_bytes=VMEM_LIMIT),
        name="dn_mixer",
    )(main, main, small, cw, dtb, alog, nw, tri, e_dn, shift)


def _rms(v):
    return v * lax.rsqrt(jnp.mean(v * v, axis=-1, keepdims=True) + NORM_EPS)


def _dense_kernel(x_ref, ys_ref, yd_ref, nw_ref, wg_ref, wb0_ref, wb1_ref, wo_ref, fnw_ref, wgu_ref,
                  wd_ref, fw_ref, o_ref):
    x = x_ref[...]
    u = (_rms(x) * nw_ref[...]).astype(BF16)
    gates = jax.nn.sigmoid(_dot(u, wg_ref[...]))
    merged = (gates[:, 0:D_MODEL] * _dot(ys_ref[...], wb0_ref[...])
              + gates[:, D_MODEL:] * _dot(yd_ref[...], wb1_ref[...]))
    h1 = x + _dot(merged.astype(BF16), wo_ref[...])
    u2 = (_rms(h1) * fnw_ref[...]).astype(BF16)
    gu = _dot(u2, wgu_ref[...])
    act = (_silu(gu[:, 0:D_FF]) * gu[:, D_FF:]).astype(BF16)
    h2 = h1 + _dot(act, wd_ref[...])
    o_ref[...] = _rms(h2) * fw_ref[...]


def _dense(x2d, ys, yd, nw, wg, wb0, wb1, wo, fnw, wgu, wd, fw, tm):
    m = x2d.shape[0]
    rows = lambda i: (i, 0)

    def resident(shape):
        return pl.BlockSpec(shape, lambda i: (0, 0), pipeline_mode=pl.Buffered(1))

    return pl.pallas_call(
        _dense_kernel,
        out_shape=jax.ShapeDtypeStruct((m, D_MODEL), F32),
        grid=(m // tm,),
        in_specs=[pl.BlockSpec((tm, D_MODEL), rows),
                  pl.BlockSpec((tm, SSD_D_INNER), rows),
                  pl.BlockSpec((tm, DN_VALUE_DIM), rows),
                  resident((1, D_MODEL)),
                  resident((D_MODEL, 2 * D_MODEL)),
                  resident((SSD_D_INNER, D_MODEL)),
                  resident((DN_VALUE_DIM, D_MODEL)),
                  resident((D_MODEL, D_MODEL)),
                  resident((1, D_MODEL)),
                  resident((D_MODEL, 2 * D_FF)),
                  resident((D_FF, D_MODEL)),
                  resident((1, D_MODEL))],
        out_specs=pl.BlockSpec((tm, D_MODEL), rows),
        compiler_params=pltpu.CompilerParams(
            dimension_semantics=("arbitrary",), vmem_limit_bytes=VMEM_LIMIT),
        name="merge_ffn",
    )(x2d, ys, yd, nw, wg, wb0, wb1, wo, fnw, wgu, wd, fw)


_O_XBC = SSD_D_INNER
_O_DT = _O_XBC + SSD_CONV_DIM
_O_QKV = _O_DT + SSD_HEADS
_O_A = _O_QKV + DN_CONV_DIM
_O_ZD = _O_A + 2 * DN_V_HEADS
_O_GATE = _O_ZD + DN_VALUE_DIM
D_IN_PROJ = _O_GATE + 2 * D_MODEL
WPREP_COLS = 512
ALL_W = 2 * D_MODEL + MAIN_W


def _wprep_sources():
    segments = ((_O_GATE, 2 * D_MODEL), (0, SSD_D_INNER), (_O_ZD, DN_VALUE_DIM), (_O_QKV, DN_CONV_DIM),
                (_O_XBC, SSD_CONV_DIM))
    return np.asarray([lo + k for lo, width in segments for k in range(0, width, WPREP_COLS)], np.int32)


def _wprep_kernel(src_ref, wt_ref, dt_ref, ab_ref, all_ref, small_ref):
    del src_ref
    all_ref[...] = wt_ref[...].T.astype(BF16)

    @pl.when(pl.program_id(0) == 0)
    def _():
        n_small = SSD_HEADS + 2 * DN_V_HEADS
        rows = jnp.concatenate([dt_ref[...], ab_ref[...], jnp.zeros((SMALL_W - n_small, D_MODEL), F32)], axis=0)
        small_ref[...] = rows.T.astype(BF16)


def _wprep(w_t):
    src = jnp.asarray(_wprep_sources())
    return pl.pallas_call(
        _wprep_kernel,
        out_shape=(jax.ShapeDtypeStruct((D_MODEL, ALL_W), BF16),
                   jax.ShapeDtypeStruct((D_MODEL, SMALL_W), BF16)),
        grid_spec=pltpu.PrefetchScalarGridSpec(
            num_scalar_prefetch=1,
            grid=(ALL_W // WPREP_COLS,),
            in_specs=[pl.BlockSpec((pl.Element(WPREP_COLS), pl.Element(D_MODEL)),
                                   lambda j, src: (pl.multiple_of(src[j], 8), 0)),
                      pl.BlockSpec((pl.Element(SSD_HEADS), pl.Element(D_MODEL)), lambda j, src: (_O_DT, 0)),
                      pl.BlockSpec((pl.Element(2 * DN_V_HEADS), pl.Element(D_MODEL)), lambda j, src: (_O_A, 0))],
            out_specs=(pl.BlockSpec((D_MODEL, WPREP_COLS), lambda j, src: (0, j)),
                       pl.BlockSpec((D_MODEL, SMALL_W), lambda j, src: (0, 0)))),
        compiler_params=pltpu.CompilerParams(
            dimension_semantics=("arbitrary",), vmem_limit_bytes=VMEM_LIMIT),
        name="inproj_weight_relayout",
    )(src, w_t, w_t, w_t)


def _pad_lanes(vec, offset):
    out = jnp.zeros((1, SMALL_W), F32)
    return lax.dynamic_update_slice(out, vec.astype(F32)[None, :], (0, offset))


def _expand_matrix(src_rows, group, heads):
    e = np.zeros((SMALL_W, heads * group), np.float32)
    for h in range(heads):
        e[src_rows + h, h * group:(h + 1) * group] = 1.0
    return e


def _shift_matrix():
    s = np.zeros(((SSD_CONV - 1) * CHUNK, CONV_HALO + CHUNK), np.float32)
    for k in range(SSD_CONV - 1):
        for t in range(CHUNK):
            s[k * CHUNK + t, CONV_HALO + t - (SSD_CONV - 1) + k] = 1.0
    return s


def _row_tile(m, target):
    t = min(target, m) // BF16_SUBLANES * BF16_SUBLANES
    while m % t:
        t -= BF16_SUBLANES
    return t


def kernel(x, meta_tokens, mix_norm_w, w_in, ssd_conv_w, ssd_conv_b, ssd_dt_bias, ssd_a_log, ssd_d,
           ssd_norm_w, dn_conv_w, dn_dt_bias, dn_a_log, dn_norm_w, w_branch, w_out, ffn_norm_w,
           w_gate_up, w_down, final_norm_w):
    batch, seq, _ = x.shape
    t_pad = PAD + N_META + seq
    assert t_pad % CHUNK == 0 and mix_norm_w.shape[0] == 1

    head = jnp.concatenate([jnp.zeros((PAD, D_MODEL), x.dtype), meta_tokens.astype(x.dtype)], axis=0)
    h_pad = jnp.concatenate([jnp.broadcast_to(head[None], (batch, CHUNK, D_MODEL)), x], axis=1)
    h_pad = h_pad.reshape(batch * t_pad, D_MODEL)

    w_all, w_small = _wprep(jnp.swapaxes(w_in[0], 0, 1))

    m_pad = batch * t_pad
    main, small = _inproj(h_pad, mix_norm_w.astype(F32), w_all, w_small,
                          _row_tile(m_pad, INPROJ_ROWS), INPROJ_COLS)

    tri = jnp.asarray(np.tile(np.tril(np.ones((CHUNK, CHUNK), np.float32)), (1, 3)), BF16)
    shift = jnp.asarray(_shift_matrix(), BF16)
    e_ssd = jnp.asarray(np.tile(_expand_matrix(COL_DT, SSD_HEAD_DIM, SSD_HEADS), (3, 1)), BF16)
    e_dn = jnp.asarray(np.tile(
        np.concatenate([_expand_matrix(COL_B, CHUNK, DN_V_HEADS) + _expand_matrix(COL_A, CHUNK, DN_V_HEADS),
                        _expand_matrix(COL_B, DN_HEAD, DN_V_HEADS) + _expand_matrix(COL_A, DN_HEAD, DN_V_HEADS)],
                       axis=1), (3, 1)), BF16)

    main = main.reshape(batch, t_pad, MAIN_W)
    small = small.reshape(batch, t_pad, SMALL_W)
    y_s = _ssd(main, small, ssd_conv_w[0].astype(F32), ssd_conv_b.astype(F32),
               _pad_lanes(ssd_dt_bias[0], COL_DT), _pad_lanes(ssd_a_log[0], COL_DT),
               jnp.repeat(ssd_d[0].astype(F32), SSD_HEAD_DIM)[None, :], ssd_norm_w.astype(F32),
               tri, e_ssd, shift)
    y_d = _dn(main, small, dn_conv_w[0].astype(F32), _pad_lanes(dn_dt_bias[0], COL_A),
              _pad_lanes(dn_a_log[0], COL_A), dn_norm_w.astype(F32), tri, e_dn, shift)

    x2d = x.reshape(batch * seq, D_MODEL)
    y_s = y_s.reshape(batch * seq, SSD_D_INNER)
    y_d = y_d.reshape(batch * seq, DN_VALUE_DIM)
    out = _dense(x2d, y_s, y_d, mix_norm_w.astype(F32), w_all, w_branch[0, 0].astype(BF16),
                 w_branch[0, 1].astype(BF16), w_out[0].astype(BF16), ffn_norm_w.astype(F32),
                 w_gate_up[0].astype(BF16), w_down[0].astype(BF16), final_norm_w.astype(F32)[None, :],
                 _row_tile(batch * seq, DENSE_ROWS))
    return out.reshape(batch, seq, D_MODEL)
```

```python
import numpy as np
import jax
import jax.numpy as jnp
from jax import lax
from jax.experimental import pallas as pl
from jax.experimental.pallas import tpu as pltpu

F32 = jnp.float32
BF16 = jnp.bfloat16

D_MODEL = 1024
CHUNK = 64
N_META = 16
PAD = CHUNK - N_META
NORM_EPS = 1e-6

SSD_D_INNER = 2048
SSD_HEAD_DIM = 64
SSD_HEADS = 32
SSD_GROUPS = 4
SSD_STATE = 128
SSD_CONV = 4
SSD_CONV_DIM = 3072
SSD_GROUP_W = SSD_D_INNER // SSD_GROUPS

DN_QK_HEADS = 8
DN_V_HEADS = 16
DN_HEAD = 128
DN_KEY_DIM = 1024
DN_VALUE_DIM = 2048
DN_CONV = 4
DN_CONV_DIM = 4096

D_FF = 2816
SMALL_W = 128
COL_DT, COL_A, COL_B = 0, 32, 48
MAIN_W = SSD_D_INNER + DN_VALUE_DIM + DN_CONV_DIM + SSD_CONV_DIM
CONV_HALO = 16
CONV_COLS = 256
VMEM_LIMIT = 56 * 1024 * 1024
BF16_SUBLANES = 16
INPROJ_ROWS, INPROJ_COLS = 2064, 1024
DENSE_ROWS = 512


def _silu(v):
    return v * jax.nn.sigmoid(v)


def _dot(a, b):
    return jnp.dot(a, b, preferred_element_type=F32)


def _dot_nt(a, b):
    return lax.dot_general(a, b, (((1,), (1,)), ((), ())), preferred_element_type=F32)


def _split3(v):
    p1 = v.astype(BF16)
    r1 = v - p1.astype(F32)
    p2 = r1.astype(BF16)
    p3 = (r1 - p2.astype(F32)).astype(BF16)
    return p1, p2, p3


def _dot_sel_right(sel3, v):
    return _dot(sel3, jnp.concatenate(_split3(v), axis=0))


def _dot_sel_left(v, sel3):
    return _dot(jnp.concatenate(_split3(v), axis=1), sel3)


def _inproj_kernel(h_ref, nw_ref, w_ref, ws_ref, o_ref, os_ref, u_scr):
    @pl.when(pl.program_id(1) == 0)
    def _():
        x = h_ref[...]
        ms = jnp.mean(x * x, axis=-1, keepdims=True)
        u = (x * lax.rsqrt(ms + NORM_EPS) * nw_ref[...]).astype(BF16)
        u_scr[...] = u
        os_ref[...] = _dot(u, ws_ref[...])

    o_ref[...] = _dot(u_scr[...], w_ref[...]).astype(BF16)


def _inproj(h_pad, norm_w, w_all, w_small, tm, tn):
    m = h_pad.shape[0]
    assert (2 * D_MODEL) % tn == 0
    return pl.pallas_call(
        _inproj_kernel,
        out_shape=(jax.ShapeDtypeStruct((m, MAIN_W), BF16),
                   jax.ShapeDtypeStruct((m, SMALL_W), F32)),
        grid=(m // tm, MAIN_W // tn),
        in_specs=[pl.BlockSpec((tm, D_MODEL), lambda i, j: (i, 0)),
                  pl.BlockSpec((1, D_MODEL), lambda i, j: (0, 0)),
                  pl.BlockSpec((D_MODEL, tn), lambda i, j: (0, j + 2 * D_MODEL // tn)),
                  pl.BlockSpec((D_MODEL, SMALL_W), lambda i, j: (0, 0))],
        out_specs=(pl.BlockSpec((tm, tn), lambda i, j: (i, j)),
                   pl.BlockSpec((tm, SMALL_W), lambda i, j: (i, 0))),
        scratch_shapes=[pltpu.VMEM((tm, D_MODEL), BF16)],
        compiler_params=pltpu.CompilerParams(
            dimension_semantics=("arbitrary", "arbitrary"), vmem_limit_bytes=VMEM_LIMIT),
        name="inproj",
    )(h_pad, norm_w, w_all, w_small)


def _conv_units(raw_refs, halo_scr, shift_ref, cw_ref, bias_ref, out_scr):
    units = []
    col = 0
    for r in raw_refs:
        for c0 in range(0, r.shape[1], CONV_COLS):
            def unit(r=r, c0=c0, cs=slice(col + c0, col + c0 + CONV_COLS)):
                xe = jnp.concatenate([halo_scr[:, cs], r[:, c0:c0 + CONV_COLS]], axis=0)
                taps = _dot(shift_ref[...], xe)
                acc = cw_ref[SSD_CONV - 1:SSD_CONV, cs] * r[:, c0:c0 + CONV_COLS].astype(F32)
                for k in range(SSD_CONV - 1):
                    acc = acc + cw_ref[k:k + 1, cs] * taps[k * CHUNK:(k + 1) * CHUNK]
                if bias_ref is not None:
                    acc = acc + bias_ref[:, cs]
                out_scr[:, cs] = _silu(acc)
            units.append(unit)
        col += r.shape[1]

    def carry():
        col = 0
        for r in raw_refs:
            halo_scr[:, col:col + r.shape[1]] = r[CHUNK - CONV_HALO:CHUNK, :]
            col += r.shape[1]

    return units, carry


def _interleave(units, stages):
    per = -(-len(units) // len(stages))
    for i, stage in enumerate(stages):
        for unit in units[i * per:(i + 1) * per]:
            unit()
        stage()


def _valid_rows(first):
    row = lax.broadcasted_iota(jnp.int32, (CHUNK, 1), 0)
    return jnp.where(jnp.logical_or(jnp.logical_not(first), row >= PAD), 1.0, 0.0).astype(F32)


def _cum_masks(width):
    li = lax.broadcasted_iota(jnp.int32, (CHUNK, width), 0)
    si = lax.broadcasted_iota(jnp.int32, (CHUNK, width), 1) & (CHUNK - 1)
    return li, si


def _store_block_diag(w_ref, p):
    pb = p.astype(BF16)
    for j in range(p.shape[1] // CHUNK):
        js = slice(j * CHUNK, (j + 1) * CHUNK)
        w_ref[js, js] = pb[:, js]


def _ssd_kernel(zs_ref, xa_ref, xb_ref, sm_ref, cw_ref, cb_ref, dtb_ref, alog_ref, dexp_ref, nw_ref,
                tri_ref, e_ref, shift_ref, y_ref, halo_scr, xbc_scr, s_scr, wbd_scr):
    nb = zs_ref.shape[0]
    first = pl.program_id(0) == 0

    @pl.when(first)
    def _():
        halo_scr[...] = jnp.zeros(halo_scr.shape, BF16)
        s_scr[...] = jnp.zeros(s_scr.shape, F32)
        wbd_scr[...] = jnp.zeros(wbd_scr.shape, BF16)

    units, carries = [], []
    for b in range(nb):
        u, c = _conv_units((xa_ref.at[b], xb_ref.at[b]), halo_scr.at[b], shift_ref, cw_ref, cb_ref,
                           xbc_scr.at[b])
        units += u
        carries.append(c)

    lane = lax.broadcasted_iota(jnp.int32, (1, SMALL_W), 1)
    is_dt = lane < SSD_HEADS
    valid = _valid_rows(first)
    li, si = _cum_masks(SSD_D_INNER)
    st = [dict() for _ in range(nb)]

    def stage_dt():
        for b in range(nb):
            dt = jnp.where(is_dt, jax.nn.softplus(sm_ref[b] + dtb_ref[...]) * valid, 0.0)
            st[b]["dt"] = dt
            st[b]["a"] = dt * jnp.where(is_dt, -jnp.exp(alog_ref[...]), 0.0)

    def stage_cumsum():
        for b in range(nb):
            st[b]["acum"] = _dot_sel_right(tri_ref[...], st[b]["a"])

    def stage_expand():
        ex = _dot_sel_left(jnp.concatenate([t for b in range(nb) for t in (st[b]["dt"], st[b]["acum"])], axis=0),
                           e_ref[...])
        for b in range(nb):
            st[b]["ex"] = ex[2 * b * CHUNK:2 * (b + 1) * CHUNK]

    def stage_decay():
        for b in range(nb):
            ac_e = st[b]["ex"][CHUNK:2 * CHUNK]
            arow = jnp.sum(jnp.where(li == si, ac_e, 0.0), axis=0, keepdims=True)
            alast = ac_e[CHUNK - 1:CHUNK, :]
            st[b].update(dt_e=st[b]["ex"][0:CHUNK],
                         lmat=jnp.exp(jnp.where(li >= si, ac_e - arow, -jnp.inf)),
                         to_end=jnp.exp(alast - ac_e), from_start=jnp.exp(ac_e),
                         chunk_decay=jnp.exp(alast))

    _interleave(units, [stage_dt, stage_cumsum, stage_expand, stage_decay])
    for c in carries:
        c()

    c_off = SSD_D_INNER + SSD_GROUPS * SSD_STATE
    for g in range(SSD_GROUPS):
        cs = slice(g * SSD_GROUP_W, (g + 1) * SSD_GROUP_W)
        for b in range(nb):
            b_g = xbc_scr[b, :, SSD_D_INNER + g * SSD_STATE:SSD_D_INNER + (g + 1) * SSD_STATE]
            c_g = xbc_scr[b, :, c_off + g * SSD_STATE:c_off + (g + 1) * SSD_STATE]
            x_g = xbc_scr[b, :, cs]
            c_b = c_g.astype(BF16)
            b_b = b_g.astype(BF16)
            cb_t = _dot_nt(c_b, jnp.concatenate([b_b] * 8, axis=0))
            scores = cb_t * st[b]["lmat"][:, cs]
            xdt = x_g * st[b]["dt_e"][:, cs]
            y_parts = []
            for q in range(2):
                qs = slice(q * 4 * CHUNK, (q + 1) * 4 * CHUNK)
                w_bd = wbd_scr.at[(b * SSD_GROUPS + g) * 2 + q]
                _store_block_diag(w_bd, xdt[:, qs])
                y_parts.append(_dot(scores[:, qs].astype(BF16), w_bd[...]))
            y_diag = jnp.concatenate(y_parts, axis=1)
            s_old = s_scr[b, :, cs]
            y_off = _dot(c_b, s_old.astype(BF16)) * st[b]["from_start"][:, cs]
            xw = (xdt * st[b]["to_end"][:, cs]).astype(BF16)
            s_scr[b, :, cs] = s_old * st[b]["chunk_decay"][:, cs] + _dot(b_g.T.astype(BF16), xw)
            y = y_diag + y_off + dexp_ref[:, cs] * x_g
            y = y * _silu(zs_ref[b, :, cs].astype(F32))
            ms = jnp.mean(y * y, axis=-1, keepdims=True)
            y_ref[b, :, cs] = (y * lax.rsqrt(ms + NORM_EPS) * nw_ref[:, cs]).astype(BF16)


def _ssd(main, small, cw, cb, dtb, alog, dexp, nw, tri, e_ssd, shift):
    batch, t_pad, _ = main.shape
    nc = t_pad // CHUNK
    x_off = SSD_D_INNER + DN_VALUE_DIM + DN_CONV_DIM
    const = lambda c: (0, 0)
    return pl.pallas_call(
        _ssd_kernel,
        out_shape=jax.ShapeDtypeStruct((batch, (nc - 1) * CHUNK, SSD_D_INNER), BF16),
        grid=(nc,),
        in_specs=[pl.BlockSpec((batch, CHUNK, SSD_D_INNER), lambda c: (0, c, 0)),
                  pl.BlockSpec((batch, CHUNK, 2048), lambda c: (0, c, x_off // 2048)),
                  pl.BlockSpec((batch, CHUNK, 1024), lambda c: (0, c, (x_off + 2048) // 1024)),
                  pl.BlockSpec((batch, CHUNK, SMALL_W), lambda c: (0, c, 0)),
                  pl.BlockSpec((SSD_CONV, SSD_CONV_DIM), const),
                  pl.BlockSpec((1, SSD_CONV_DIM), const),
                  pl.BlockSpec((1, SMALL_W), const),
                  pl.BlockSpec((1, SMALL_W), const),
                  pl.BlockSpec((1, SSD_D_INNER), const),
                  pl.BlockSpec((1, SSD_D_INNER), const),
                  pl.BlockSpec((CHUNK, 3 * CHUNK), const),
                  pl.BlockSpec((3 * SMALL_W, SSD_D_INNER), const),
                  pl.BlockSpec(shift.shape, const)],
        out_specs=pl.BlockSpec((batch, CHUNK, SSD_D_INNER), lambda c: (0, jnp.maximum(c - 1, 0), 0)),
        scratch_shapes=[pltpu.VMEM((batch, CONV_HALO, SSD_CONV_DIM), BF16),
                        pltpu.VMEM((batch, CHUNK, SSD_CONV_DIM), F32),
                        pltpu.VMEM((batch, SSD_STATE, SSD_D_INNER), F32),
                        pltpu.VMEM((batch * SSD_HEADS // 4, 4 * CHUNK, 4 * CHUNK), BF16)],
        compiler_params=pltpu.CompilerParams(
            dimension_semantics=("arbitrary",), vmem_limit_bytes=VMEM_LIMIT),
        name="ssd_mixer",
    )(main, main, main, small, cw, cb, dtb, alog, dexp, nw, tri, e_ssd, shift)


def _l2norm(v):
    return v * lax.rsqrt(jnp.sum(v * v, axis=-1, keepdims=True) + NORM_EPS)


def _dn_kernel(zd_ref, qkv_ref, sm_ref, cw_ref, dtb_ref, alog_ref, nw_ref, tri_ref, e_ref, shift_ref,
               y_ref, halo_scr, qkv_scr, s_scr, wbd_scr):
    nb = zd_ref.shape[0]
    first = pl.program_id(0) == 0

    @pl.when(first)
    def _():
        halo_scr[...] = jnp.zeros(halo_scr.shape, BF16)
        s_scr[...] = jnp.zeros(s_scr.shape, F32)
        wbd_scr[...] = jnp.zeros(wbd_scr.shape, BF16)

    units, carries = [], []
    for b in range(nb):
        u, c = _conv_units((qkv_ref.at[b],), halo_scr.at[b], shift_ref, cw_ref, None, qkv_scr.at[b])
        units += u
        carries.append(c)

    lane = lax.broadcasted_iota(jnp.int32, (1, SMALL_W), 1)
    is_a = jnp.logical_and(lane >= COL_A, lane < COL_A + DN_V_HEADS)
    is_b = jnp.logical_and(lane >= COL_B, lane < COL_B + DN_V_HEADS)
    valid = _valid_rows(first)
    w64 = DN_V_HEADS * CHUNK
    li, si = _cum_masks(w64)
    gates = [dict() for _ in range(nb)]

    def stage_gate():
        for b in range(nb):
            sm = sm_ref[b]
            gates[b]["beta"] = jnp.where(is_b, jax.nn.sigmoid(sm) * valid, 0.0)
            neg_a = jnp.where(is_a, -jnp.exp(alog_ref[...]), 0.0)
            gates[b]["g"] = neg_a * jax.nn.softplus(sm + dtb_ref[...]) * valid

    def stage_cumsum():
        for b in range(nb):
            gates[b]["gcum"] = _dot_sel_right(tri_ref[...], gates[b]["g"])

    def stage_expand():
        ex = _dot_sel_left(jnp.concatenate([t for b in range(nb) for t in (gates[b]["beta"], gates[b]["gcum"])],
                                           axis=0), e_ref[...])
        for b in range(nb):
            gates[b]["ex"] = ex[2 * b * CHUNK:2 * (b + 1) * CHUNK]

    def stage_decay():
        for b in range(nb):
            ex = gates[b]["ex"]
            gc64 = ex[CHUNK:, 0:w64]
            gc128 = ex[CHUNK:, w64:]
            grow = jnp.sum(jnp.where(li == si, gc64, 0.0), axis=0, keepdims=True)
            glast = gc128[CHUNK - 1:CHUNK, :]
            gates[b].update(beta64=ex[0:CHUNK, 0:w64], beta128=ex[0:CHUNK, w64:],
                            dec=jnp.exp(jnp.where(li >= si, gc64 - grow, -jnp.inf)),
                            from_start=jnp.exp(gc128), to_end=jnp.exp(glast - gc128),
                            chunk_decay=jnp.exp(glast))

    _interleave(units, [stage_gate, stage_cumsum, stage_expand, stage_decay])
    for c in carries:
        c()

    li2, si2 = _cum_masks(2 * CHUNK)
    strict = li2 > si2
    eye2 = jnp.where(li2 == si2, 1.0, 0.0).astype(F32)

    q_heads, k_heads, n_mats, qkds = [], [], [], []
    for b in range(nb):
        for i in range(DN_QK_HEADS):
            q_i = _l2norm(qkv_scr[b, :, i * DN_HEAD:(i + 1) * DN_HEAD]) * (DN_HEAD ** -0.5)
            k_i = _l2norm(qkv_scr[b, :, DN_KEY_DIM + i * DN_HEAD:DN_KEY_DIM + (i + 1) * DN_HEAD])
            k_b = k_i.astype(BF16)
            prod = _dot_nt(jnp.concatenate([q_i.astype(BF16), k_b], axis=0),
                           jnp.concatenate([k_b, k_b], axis=0))
            ps = slice(i * 2 * CHUNK, (i + 1) * 2 * CHUNK)
            d_pair = gates[b]["dec"][:, ps]
            qkds.append(prod[0:CHUNK] * d_pair)
            n_mats.append(jnp.where(strict, -(gates[b]["beta64"][:, ps] * prod[CHUNK:] * d_pair), 0.0))
            q_heads.append(q_i)
            k_heads.append(k_i)

    pairs = range(nb * DN_QK_HEADS)
    t_mats = [eye2 + n for n in n_mats]

    def times_block_diag(lhs, p_list):
        for q in pairs:
            _store_block_diag(wbd_scr.at[q], p_list[q])
        return [_dot(lhs[q].astype(BF16), wbd_scr[q]) for q in pairs]

    p_mats = times_block_diag(n_mats, n_mats)
    for _ in range(4):
        boths = times_block_diag([jnp.concatenate([p_mats[q], t_mats[q]], axis=0) for q in pairs], p_mats)
        p_mats = [b[0:CHUNK] for b in boths]
        t_mats = [t + b[CHUNK:] for t, b in zip(t_mats, boths)]
    t_mats = [t + d for t, d in zip(t_mats, times_block_diag(t_mats, p_mats))]

    heads = [(b, h) for b in range(nb) for h in range(DN_V_HEADS)]
    hsl = [slice(h * DN_HEAD, (h + 1) * DN_HEAD) for h in range(DN_V_HEADS)]
    uws = []
    for n, (b, h) in enumerate(heads):
        k_i = k_heads[n // 2]
        v_h = qkv_scr[b, :, 2 * DN_KEY_DIM + h * DN_HEAD:2 * DN_KEY_DIM + (h + 1) * DN_HEAD]
        b_h = gates[b]["beta128"][:, hsl[h]]
        rhs = jnp.concatenate([v_h * b_h, k_i * (b_h * gates[b]["from_start"][:, hsl[h]])], axis=1)
        j = h % 2
        t_h = t_mats[n // 2][:, j * CHUNK:(j + 1) * CHUNK].astype(BF16)
        uws.append(_dot(t_h, rhs.astype(BF16)))

    s_olds = [s_scr[n] for n in range(len(heads))]
    ws_qs = [_dot(jnp.concatenate([uws[n][:, DN_HEAD:], q_heads[n // 2] * gates[b]["from_start"][:, hsl[h]]],
                                  axis=0).astype(BF16), s_olds[n].astype(BF16))
             for n, (b, h) in enumerate(heads)]
    v_news = [(uws[n][:, 0:DN_HEAD] - ws_qs[n][0:CHUNK]).astype(BF16) for n in range(len(heads))]
    for n, (b, h) in enumerate(heads):
        k_dec = k_heads[n // 2] * gates[b]["to_end"][:, hsl[h]]
        s_scr[n] = s_olds[n] * gates[b]["chunk_decay"][:, hsl[h]] + _dot(k_dec.T.astype(BF16), v_news[n])
        j = h % 2
        o_h = ws_qs[n][CHUNK:] + _dot(qkds[n // 2][:, j * CHUNK:(j + 1) * CHUNK].astype(BF16), v_news[n])
        ms = jnp.mean(o_h * o_h, axis=-1, keepdims=True)
        y = o_h * lax.rsqrt(ms + NORM_EPS) * nw_ref[...]
        y_ref[b, :, hsl[h]] = (y * _silu(zd_ref[b, :, hsl[h]].astype(F32))).astype(BF16)


def _dn(main, small, cw, dtb, alog, nw, tri, e_dn, shift):
    batch, t_pad, _ = main.shape
    nc = t_pad // CHUNK
    const = lambda c: (0, 0)
    return pl.pallas_call(
        _dn_kernel,
        out_shape=jax.ShapeDtypeStruct((batch, (nc - 1) * CHUNK, DN_VALUE_DIM), BF16),
        grid=(nc,),
        in_specs=[pl.BlockSpec((batch, CHUNK, DN_VALUE_DIM), lambda c: (0, c, SSD_D_INNER // DN_VALUE_DIM)),
                  pl.BlockSpec((batch, CHUNK, DN_CONV_DIM),
                               lambda c: (0, c, (SSD_D_INNER + DN_VALUE_DIM) // DN_CONV_DIM)),
                  pl.BlockSpec((batch, CHUNK, SMALL_W), lambda c: (0, c, 0)),
                  pl.BlockSpec((DN_CONV, DN_CONV_DIM), const),
                  pl.BlockSpec((1, SMALL_W), const),
                  pl.BlockSpec((1, SMALL_W), const),
                  pl.BlockSpec((1, DN_HEAD), const),
                  pl.BlockSpec((CHUNK, 3 * CHUNK), const),
                  pl.BlockSpec(e_dn.shape, const),
                  pl.BlockSpec(shift.shape, const)],
        out_specs=pl.BlockSpec((batch, CHUNK, DN_VALUE_DIM), lambda c: (0, jnp.maximum(c - 1, 0), 0)),
        scratch_shapes=[pltpu.VMEM((batch, CONV_HALO, DN_CONV_DIM), BF16),
                        pltpu.VMEM((batch, CHUNK, DN_CONV_DIM), F32),
                        pltpu.VMEM((batch * DN_V_HEADS, DN_HEAD, DN_HEAD), F32),
                        pltpu.VMEM((batch * DN_QK_HEADS, 2 * CHUNK, 2 * CHUNK), BF16)],
        compiler_params=pltpu.CompilerParams(
            dimension_semantics=("arbitrary",), vmem_limit_bytes=VMEM_LIMIT),
        name="dn_mixer",
    )(main, main, small, cw, dtb, alog, nw, tri, e_dn, shift)


def _rms(v):
    return v * lax.rsqrt(jnp.mean(v * v, axis=-1, keepdims=True) + NORM_EPS)


def _dense_kernel(x_ref, ys_ref, yd_ref, nw_ref, wg_ref, wb0_ref, wb1_ref, wo_ref, fnw_ref, wgu_ref,
                  wd_ref, fw_ref, o_ref):
    x = x_ref[...]
    u = (_rms(x) * nw_ref[...]).astype(BF16)
    gates = jax.nn.sigmoid(_dot(u, wg_ref[...]))
    merged = (gates[:, 0:D_MODEL] * _dot(ys_ref[...], wb0_ref[...])
              + gates[:, D_MODEL:] * _dot(yd_ref[...], wb1_ref[...]))
    h1 = x + _dot(merged.astype(BF16), wo_ref[...])
    u2 = (_rms(h1) * fnw_ref[...]).astype(BF16)
    gu = _dot(u2, wgu_ref[...])
    act = (_silu(gu[:, 0:D_FF]) * gu[:, D_FF:]).astype(BF16)
    h2 = h1 + _dot(act, wd_ref[...])
    o_ref[...] = _rms(h2) * fw_ref[...]


def _dense(x2d, ys, yd, nw, wg, wb0, wb1, wo, fnw, wgu, wd, fw, tm):
    m = x2d.shape[0]
    rows = lambda i: (i, 0)

    def resident(shape):
        return pl.BlockSpec(shape, lambda i: (0, 0), pipeline_mode=pl.Buffered(1))

    return pl.pallas_call(
        _dense_kernel,
        out_shape=jax.ShapeDtypeStruct((m, D_MODEL), F32),
        grid=(m // tm,),
        in_specs=[pl.BlockSpec((tm, D_MODEL), rows),
                  pl.BlockSpec((tm, SSD_D_INNER), rows),
                  pl.BlockSpec((tm, DN_VALUE_DIM), rows),
                  resident((1, D_MODEL)),
                  resident((D_MODEL, 2 * D_MODEL)),
                  resident((SSD_D_INNER, D_MODEL)),
                  resident((DN_VALUE_DIM, D_MODEL)),
                  resident((D_MODEL, D_MODEL)),
                  resident((1, D_MODEL)),
                  resident((D_MODEL, 2 * D_FF)),
                  resident((D_FF, D_MODEL)),
                  resident((1, D_MODEL))],
        out_specs=pl.BlockSpec((tm, D_MODEL), rows),
        compiler_params=pltpu.CompilerParams(
            dimension_semantics=("arbitrary",), vmem_limit_bytes=VMEM_LIMIT),
        name="merge_ffn",
    )(x2d, ys, yd, nw, wg, wb0, wb1, wo, fnw, wgu, wd, fw)


_O_XBC = SSD_D_INNER
_O_DT = _O_XBC + SSD_CONV_DIM
_O_QKV = _O_DT + SSD_HEADS
_O_A = _O_QKV + DN_CONV_DIM
_O_ZD = _O_A + 2 * DN_V_HEADS
_O_GATE = _O_ZD + DN_VALUE_DIM
D_IN_PROJ = _O_GATE + 2 * D_MODEL
WPREP_COLS = 512
ALL_W = 2 * D_MODEL + MAIN_W


def _wprep_sources():
    segments = ((_O_GATE, 2 * D_MODEL), (0, SSD_D_INNER), (_O_ZD, DN_VALUE_DIM), (_O_QKV, DN_CONV_DIM),
                (_O_XBC, SSD_CONV_DIM))
    return np.asarray([lo + k for lo, width in segments for k in range(0, width, WPREP_COLS)], np.int32)


def _wprep_kernel(src_ref, wt_ref, dt_ref, ab_ref, all_ref, small_ref):
    del src_ref
    all_ref[...] = wt_ref[...].T.astype(BF16)

    @pl.when(pl.program_id(0) == 0)
    def _():
        n_small = SSD_HEADS + 2 * DN_V_HEADS
        rows = jnp.concatenate([dt_ref[...], ab_ref[...], jnp.zeros((SMALL_W - n_small, D_MODEL), F32)], axis=0)
        small_ref[...] = rows.T.astype(BF16)


def _wprep(w_t):
    src = jnp.asarray(_wprep_sources())
    return pl.pallas_call(
        _wprep_kernel,
        out_shape=(jax.ShapeDtypeStruct((D_MODEL, ALL_W), BF16),
                   jax.ShapeDtypeStruct((D_MODEL, SMALL_W), BF16)),
        grid_spec=pltpu.PrefetchScalarGridSpec(
            num_scalar_prefetch=1,
            grid=(ALL_W // WPREP_COLS,),
            in_specs=[pl.BlockSpec((pl.Element(WPREP_COLS), pl.Element(D_MODEL)),
                                   lambda j, src: (pl.multiple_of(src[j], 8), 0)),
                      pl.BlockSpec((pl.Element(SSD_HEADS), pl.Element(D_MODEL)), lambda j, src: (_O_DT, 0)),
                      pl.BlockSpec((pl.Element(2 * DN_V_HEADS), pl.Element(D_MODEL)), lambda j, src: (_O_A, 0))],
            out_specs=(pl.BlockSpec((D_MODEL, WPREP_COLS), lambda j, src: (0, j)),
                       pl.BlockSpec((D_MODEL, SMALL_W), lambda j, src: (0, 0)))),
        compiler_params=pltpu.CompilerParams(
            dimension_semantics=("arbitrary",), vmem_limit_bytes=VMEM_LIMIT),
        name="inproj_weight_relayout",
    )(src, w_t, w_t, w_t)


def _pad_lanes(vec, offset):
    out = jnp.zeros((1, SMALL_W), F32)
    return lax.dynamic_update_slice(out, vec.astype(F32)[None, :], (0, offset))


def _expand_matrix(src_rows, group, heads):
    e = np.zeros((SMALL_W, heads * group), np.float32)
    for h in range(heads):
        e[src_rows + h, h * group:(h + 1) * group] = 1.0
    return e


def _shift_matrix():
    s = np.zeros(((SSD_CONV - 1) * CHUNK, CONV_HALO + CHUNK), np.float32)
    for k in range(SSD_CONV - 1):
        for t in range(CHUNK):
            s[k * CHUNK + t, CONV_HALO + t - (SSD_CONV - 1) + k] = 1.0
    return s


def _row_tile(m, target):
    t = min(target, m) // BF16_SUBLANES * BF16_SUBLANES
    while m % t:
        t -= BF16_SUBLANES
    return t


def kernel(x, meta_tokens, mix_norm_w, w_in, ssd_conv_w, ssd_conv_b, ssd_dt_bias, ssd_a_log, ssd_d,
           ssd_norm_w, dn_conv_w, dn_dt_bias, dn_a_log, dn_norm_w, w_branch, w_out, ffn_norm_w,
           w_gate_up, w_down, final_norm_w):
    batch, seq, _ = x.shape
    t_pad = PAD + N_META + seq
    assert t_pad % CHUNK == 0 and mix_norm_w.shape[0] == 1

    head = jnp.concatenate([jnp.zeros((PAD, D_MODEL), x.dtype), meta_tokens.astype(x.dtype)], axis=0)
    h_pad = jnp.concatenate([jnp.broadcast_to(head[None], (batch, CHUNK, D_MODEL)), x], axis=1)
    h_pad = h_pad.reshape(batch * t_pad, D_MODEL)

    w_all, w_small = _wprep(jnp.swapaxes(w_in[0], 0, 1))

    m_pad = batch * t_pad
    main, small = _inproj(h_pad, mix_norm_w.astype(F32), w_all, w_small,
                          _row_tile(m_pad, INPROJ_ROWS), INPROJ_COLS)

    tri = jnp.asarray(np.tile(np.tril(np.ones((CHUNK, CHUNK), np.float32)), (1, 3)), BF16)
    shift = jnp.asarray(_shift_matrix(), BF16)
    e_ssd = jnp.asarray(np.tile(_expand_matrix(COL_DT, SSD_HEAD_DIM, SSD_HEADS), (3, 1)), BF16)
    e_dn = jnp.asarray(np.tile(
        np.concatenate([_expand_matrix(COL_B, CHUNK, DN_V_HEADS) + _expand_matrix(COL_A, CHUNK, DN_V_HEADS),
                        _expand_matrix(COL_B, DN_HEAD, DN_V_HEADS) + _expand_matrix(COL_A, DN_HEAD, DN_V_HEADS)],
                       axis=1), (3, 1)), BF16)

    main = main.reshape(batch, t_pad, MAIN_W)
    small = small.reshape(batch, t_pad, SMALL_W)
    y_s = _ssd(main, small, ssd_conv_w[0].astype(F32), ssd_conv_b.astype(F32),
               _pad_lanes(ssd_dt_bias[0], COL_DT), _pad_lanes(ssd_a_log[0], COL_DT),
               jnp.repeat(ssd_d[0].astype(F32), SSD_HEAD_DIM)[None, :], ssd_norm_w.astype(F32),
               tri, e_ssd, shift)
    y_d = _dn(main, small, dn_conv_w[0].astype(F32), _pad_lanes(dn_dt_bias[0], COL_A),
              _pad_lanes(dn_a_log[0], COL_A), dn_norm_w.astype(F32), tri, e_dn, shift)

    x2d = x.reshape(batch * seq, D_MODEL)
    y_s = y_s.reshape(batch * seq, SSD_D_INNER)
    y_d = y_d.reshape(batch * seq, DN_VALUE_DIM)
    out = _dense(x2d, y_s, y_d, mix_norm_w.astype(F32), w_all, w_branch[0, 0].astype(BF16),
                 w_branch[0, 1].astype(BF16), w_out[0].astype(BF16), ffn_norm_w.astype(F32),
                 w_gate_up[0].astype(BF16), w_down[0].astype(BF16), final_norm_w.astype(F32)[None, :],
                 _row_tile(batch * seq, DENSE_ROWS))
    return out.reshape(batch, seq, D_MODEL)
```
